```python
import math
import jax, jax.numpy as jnp
from jax import lax
import numpy as np

D_MODEL = 2048
BATCH = 4
SEQ = 4096
DEPTH = 2

SB_HEADS = 8
SB_HEAD_DIM = 128
SB_WIDTH = SB_HEADS * SB_HEAD_DIM
POOL_WINDOWS = (2, 4, 8, 16)
POOL_GROUPS = len(POOL_WINDOWS)
POOL_WIDTH = D_MODEL - SB_WIDTH
POOL_GROUP_DIM = POOL_WIDTH // POOL_GROUPS
MIX_WIDTH = SB_WIDTH + POOL_WIDTH
EVEN_IN = 4 * SB_WIDTH + 2 * POOL_WIDTH
BLOCK_Q = 128

RWKV_HEAD_DIM = 64
RWKV_HEADS = D_MODEL // RWKV_HEAD_DIM
DECAY_LORA = 96
ICLR_LORA = 96
GN_EPS = 64e-5
L2_EPS = 1e-12

LN_EPS = 1e-5
DEEPNORM_ALPHA = (2 * DEPTH) ** 0.25
DEEPNORM_BETA = (8 * DEPTH) ** -0.25
N_EVEN = (DEPTH + 1) // 2
N_ODD = DEPTH // 2

kernel_name = "hybrid_stickbreak_pool_rwkv7"


def layer_norm(x, g, b):
    xf = x.astype(jnp.float32)
    mean = jnp.mean(xf, axis=-1, keepdims=True)
    var = jnp.mean(jnp.square(xf - mean), axis=-1, keepdims=True)
    return ((xf - mean) * lax.rsqrt(var + LN_EPS) * g + b).astype(x.dtype)


def stick_breaking_attention(q, k, v):
    S = q.shape[2]
    scale = SB_HEAD_DIM ** -0.5
    outs = []
    for start in range(0, S, BLOCK_Q):
        end = start + BLOCK_Q
        qb = q[:, :, start:end].astype(jnp.float32)
        kb = k[:, :, :end].astype(jnp.float32)
        vb = v[:, :, :end]
        z = jnp.einsum('bhqd,bhkd->bhqk', qb, kb) * scale
        t_idx = start + jnp.arange(BLOCK_Q)[:, None]
        s_idx = jnp.arange(end)[None, :]
        causal = s_idx < t_idx
        log_keep = jnp.where(causal, jax.nn.log_sigmoid(-z), 0.0)
        later = lax.cumsum(log_keep, axis=3, reverse=True) - log_keep
        weights = jnp.where(causal, jnp.exp(jax.nn.log_sigmoid(z) + later), 0.0)
        outs.append(jnp.einsum('bhqk,bhkd->bhqd', weights.astype(vb.dtype), vb))
    return jnp.concatenate(outs, axis=2)


def multiscale_pool(u, w_pool, pool_scale):
    B_, S, _ = u.shape
    ug = u.reshape(B_, S, POOL_GROUPS, POOL_GROUP_DIM).astype(jnp.float32)
    c0 = jnp.concatenate([jnp.zeros_like(ug[:, :1]), jnp.cumsum(ug, axis=1)], axis=1)
    pos = jnp.arange(1, S + 1, dtype=jnp.float32)
    outs = []
    for g, w in enumerate(POOL_WINDOWS):
        cg = c0[:, :, g]
        lower = jnp.concatenate([jnp.zeros_like(cg[:, :w - 1]), cg[:, :S - w + 1]], axis=1)
        count = jnp.minimum(pos, float(w))[None, :, None]
        outs.append((cg[:, 1:] - lower) / count - ug[:, :, g])
    pooled = jnp.stack(outs, axis=2)
    mixed = jnp.einsum('bsgc,gcd->bsgd', pooled, w_pool.astype(jnp.float32))
    return (mixed.reshape(B_, S, POOL_WIDTH) * pool_scale).astype(u.dtype)


def even_layer(x, w_in, w_pool, pool_scale, w_out):
    B_, S, _ = x.shape
    h = x @ w_in
    q, k, v, g_a, u, g_b = jnp.split(
        h, [SB_WIDTH, 2 * SB_WIDTH, 3 * SB_WIDTH, 4 * SB_WIDTH, 4 * SB_WIDTH + POOL_WIDTH], axis=-1)

    def heads(t):
        return t.reshape(B_, S, SB_HEADS, SB_HEAD_DIM).transpose(0, 2, 1, 3)

    o_a = stick_breaking_attention(heads(q), heads(k), heads(v))
    o_a = o_a.transpose(0, 2, 1, 3).reshape(B_, S, SB_WIDTH)
    o_b = multiscale_pool(u, w_pool, pool_scale)
    mixed = jnp.concatenate([o_a * jax.nn.silu(g_a), o_b * jax.nn.silu(g_b)], axis=-1)
    return mixed @ w_out


def wkv7_scan(r, w, k, v, kk, a):
    B_, S, H, N = r.shape

    def step(state, inp):
        r_t, w_t, k_t, v_t, kk_t, a_t = inp
        sa = jnp.einsum('bhvk,bhk->bhv', state, -kk_t)
        state = (state * w_t[:, :, None, :]
                 + sa[..., None] * (kk_t * a_t)[:, :, None, :]
                 + v_t[..., None] * k_t[:, :, None, :])
        return state, jnp.einsum('bhvk,bhk->bhv', state, r_t)

    xs = (jnp.moveaxis(r, 1, 0), jnp.moveaxis(w, 1, 0), jnp.moveaxis(k, 1, 0),
          jnp.moveaxis(v, 1, 0), jnp.moveaxis(kk, 1, 0), jnp.moveaxis(a, 1, 0))
    state0 = jnp.zeros((B_, H, N, N), jnp.float32)
    _, out = lax.scan(step, state0, xs)
    return jnp.moveaxis(out, 0, 1)


def odd_layer(x, mu, w_r, w_k, w_v, w_g, w0, w1, w2, a0, a1, a2, k_k, k_a, r_k, gn_w, gn_b, w_o):
    B_, S, D = x.shape
    H, N = RWKV_HEADS, RWKV_HEAD_DIM
    f32 = jnp.float32
    x_prev = jnp.pad(x, ((0, 0), (1, 0), (0, 0)))[:, :S]
    xx = x_prev - x
    xr = x + xx * mu[0]
    xw = x + xx * mu[1]
    xk = x + xx * mu[2]
    xv = x + xx * mu[3]
    xa = x + xx * mu[4]
    xg = x + xx * mu[5]
    r = (xr @ w_r).astype(f32)
    k = (xk @ w_k).astype(f32)
    v = (xv @ w_v).astype(f32)
    g = (xg @ w_g).astype(f32)
    w_log = -jax.nn.softplus(-(w0 + jnp.tanh(xw @ w1) @ w2).astype(f32)) - 0.5
    decay = jnp.exp(-jnp.exp(w_log))
    a = jax.nn.sigmoid((a0 + (xa @ a1) @ a2).astype(f32))
    kk = (k * k_k).reshape(B_, S, H, N)
    kk = kk / jnp.maximum(jnp.sqrt(jnp.sum(kk * kk, axis=-1, keepdims=True)), L2_EPS)
    k = k * (1.0 + (a - 1.0) * k_a)
    r = r.reshape(B_, S, H, N)
    k = k.reshape(B_, S, H, N)
    v = v.reshape(B_, S, H, N)
    o = wkv7_scan(r, decay.reshape(B_, S, H, N), k, v, kk, a.reshape(B_, S, H, N))
    mean = jnp.mean(o, axis=-1, keepdims=True)
    var = jnp.mean(jnp.square(o - mean), axis=-1, keepdims=True)
    o = ((o - mean) * lax.rsqrt(var + GN_EPS)).reshape(B_, S, D) * gn_w + gn_b
    bonus = jnp.sum(r * k * r_k, axis=-1, keepdims=True) * v
    o = (o + bonus.reshape(B_, S, D)) * jax.nn.silu(g)
    return o.astype(x.dtype) @ w_o


def setup_inputs(seed: int = 0) -> dict:
    key = jax.random.key(seed)
    ks = list(jax.random.split(key, 32))
    f32 = jnp.float32

    def nrm(i, shape, scale):
        return scale * jax.random.normal(ks[i], shape, f32)

    def unif(i, shape, lo, hi):
        return jax.random.uniform(ks[i], shape, f32, lo, hi)

    D = D_MODEL
    H, N = RWKV_HEADS, RWKV_HEAD_DIM
    return {
        'x': nrm(0, (BATCH, SEQ, D), 1.0),
        'ev_w_in': nrm(1, (N_EVEN, D, EVEN_IN), D ** -0.5),
        'ev_w_pool': nrm(2, (N_EVEN, POOL_GROUPS, POOL_GROUP_DIM, POOL_GROUP_DIM), POOL_GROUP_DIM ** -0.5),
        'ev_pool_scale': 1.0 + nrm(3, (N_EVEN, POOL_WIDTH), 0.1),
        'ev_w_out': nrm(4, (N_EVEN, MIX_WIDTH, D), DEEPNORM_BETA * MIX_WIDTH ** -0.5),
        'od_mu': unif(5, (N_ODD, 6, D), 0.0, 1.0),
        'od_w_r': nrm(6, (N_ODD, D, D), D ** -0.5),
        'od_w_k': nrm(7, (N_ODD, D, D), D ** -0.5),
        'od_w_v': nrm(8, (N_ODD, D, D), D ** -0.5),
        'od_w_g': nrm(9, (N_ODD, D, D), D ** -0.5),
        'od_w0': unif(10, (N_ODD, D), -6.0, 1.0),
        'od_w1': nrm(11, (N_ODD, D, DECAY_LORA), D ** -0.5),
        'od_w2': nrm(12, (N_ODD, DECAY_LORA, D), 0.5 * DECAY_LORA ** -0.5),
        'od_a0': nrm(13, (N_ODD, D), 0.1),
        'od_a1': nrm(14, (N_ODD, D, ICLR_LORA), D ** -0.5),
        'od_a2': nrm(15, (N_ODD, ICLR_LORA, D), 0.5 * ICLR_LORA ** -0.5),
        'od_k_k': 0.85 + nrm(16, (N_ODD, D), 0.05),
        'od_k_a': 1.0 + nrm(17, (N_ODD, D), 0.05),
        'od_r_k': nrm(18, (N_ODD, H, N), 0.1),
        'od_gn_w': 1.0 + nrm(19, (N_ODD, D), 0.05),
        'od_gn_b': nrm(20, (N_ODD, D), 0.02),
        'od_w_o': nrm(21, (N_ODD, D, D), DEEPNORM_BETA * D ** -0.5),
        'ln_g': 1.0 + nrm(22, (DEPTH, D), 0.02),
        'ln_b': nrm(23, (DEPTH, D), 0.02),
    }


def reference(x, ev_w_in, ev_w_pool, ev_pool_scale, ev_w_out,
              od_mu, od_w_r, od_w_k, od_w_v, od_w_g, od_w0, od_w1, od_w2,
              od_a0, od_a1, od_a2, od_k_k, od_k_a, od_r_k, od_gn_w, od_gn_b, od_w_o,
              ln_g, ln_b):
    for layer in range(DEPTH):
        j = layer // 2
        if layer % 2 == 0:
            y = even_layer(x, ev_w_in[j], ev_w_pool[j], ev_pool_scale[j], ev_w_out[j])
        else:
            y = odd_layer(x, od_mu[j], od_w_r[j], od_w_k[j], od_w_v[j], od_w_g[j],
                          od_w0[j], od_w1[j], od_w2[j], od_a0[j], od_a1[j], od_a2[j],
                          od_k_k[j], od_k_a[j], od_r_k[j], od_gn_w[j], od_gn_b[j], od_w_o[j])
        x = layer_norm(DEEPNORM_ALPHA * x + y, ln_g[layer], ln_b[layer])
    return x
```

```python
import functools

import jax
import jax.numpy as jnp
from jax import lax
from jax.experimental import pallas as pl
from jax.experimental.pallas import tpu as pltpu

F32 = jnp.float32
BF16 = jnp.bfloat16

D_MODEL = 2048
N_LAYERS = 2
SB_HEADS = 8
SB_HEAD_DIM = 128
SB_WIDTH = SB_HEADS * SB_HEAD_DIM
POOL_WINDOWS = (2, 4, 8, 16)
POOL_GROUP_DIM = 256
POOL_WIDTH = len(POOL_WINDOWS) * POOL_GROUP_DIM
POOL_HALO = 16
RWKV_HEAD_DIM = 64
LORA_PAD = 128
GN_EPS = 64e-5
L2_EPS = 1e-12
LN_EPS = 1e-5
DEEPNORM_ALPHA = (2 * N_LAYERS) ** 0.25

SUBLANES = 8
LANES = 128
VMEM_LIMIT = 52 * 1024 * 1024

EXP_UNDERFLOW = -104.0

WKV_CHUNK = 64
WKV_PAIR = 2 * WKV_CHUNK


def _dot(a, b):
    return jnp.dot(a.astype(BF16), b.astype(BF16), preferred_element_type=F32)


def _dot_nt(a, b):
    return lax.dot_general(a.astype(BF16), b.astype(BF16), (((1,), (1,)), ((), ())),
                           preferred_element_type=F32)


def _dot_tn(a, b):
    return lax.dot_general(a.astype(BF16), b.astype(BF16), (((0,), (0,)), ((), ())),
                           preferred_element_type=F32)


def _silu(g):
    return g / (1.0 + jnp.exp(-g))


def _softplus(x):
    return jnp.maximum(x, 0.0) + jnp.log1p(jnp.exp(-jnp.abs(x)))


def _layer_norm_rows(z, g, b):
    mean = jnp.mean(z, axis=-1, keepdims=True)
    d = z - mean
    var = jnp.mean(d * d, axis=-1, keepdims=True)
    return d * lax.rsqrt(var + LN_EPS) * g + b


def _params(*sem):
    return pltpu.CompilerParams(dimension_semantics=sem, vmem_limit_bytes=VMEM_LIMIT)


def _proj_kernel(x_ref, w_ref, o_ref, xb_ref):
    @pl.when(pl.program_id(1) == 0)
    def _():
        xb_ref[...] = x_ref[...].astype(BF16)

    o_ref[...] = jnp.dot(xb_ref[...], w_ref[...], preferred_element_type=F32).astype(o_ref.dtype)


def _project(x, w, out_dtype, tm, tn):
    m, k = x.shape
    n = w.shape[1]
    return pl.pallas_call(
        _proj_kernel,
        grid=(m // tm, n // tn),
        in_specs=[pl.BlockSpec((tm, k), lambda i, j: (i, 0)),
                  pl.BlockSpec((k, tn), lambda i, j: (0, j))],
        out_specs=pl.BlockSpec((tm, tn), lambda i, j: (i, j)),
        out_shape=jax.ShapeDtypeStruct((m, n), out_dtype),
        scratch_shapes=[pltpu.VMEM((tm, k), BF16)],
        compiler_params=_params("parallel", "arbitrary"),
        name="in_proj",
    )(x, w)


def _attn_kernel(q_ref, k_ref, v_ref, g_ref, o_ref, *, tq, scale):
    i = pl.program_id(2)
    q = q_ref[0]
    row = lax.broadcasted_iota(jnp.int32, (tq, tq), 0)
    col = lax.broadcasted_iota(jnp.int32, (tq, tq), 1)
    causal = col < row
    after = jnp.where(row > col, 1.0, 0.0).astype(BF16)

    def block(j, carry, acc, diagonal):
        start = pl.multiple_of(j * tq, tq)
        kb = k_ref[0, pl.ds(start, tq), :]
        vb = v_ref[0, pl.ds(start, tq), :]
        z = _dot_nt(q, kb) * scale
        t = jnp.log1p(jnp.exp(-jnp.abs(z)))
        log_keep = -(jnp.maximum(z, 0.0) + t)
        log_beta = jnp.minimum(z, 0.0) - t
        if diagonal:
            log_keep = jnp.where(causal, log_keep, 0.0)
        hi = log_keep.astype(BF16)
        lo = (log_keep - hi.astype(F32)).astype(BF16)
        later = (jnp.dot(hi, after, preferred_element_type=F32)
                 + jnp.dot(lo, after, preferred_element_type=F32) + carry)
        w = jnp.exp(log_beta + later)
        if diagonal:
            w = jnp.where(causal, w, 0.0)
        acc = acc + _dot(w, vb)
        carry = carry + jnp.sum(log_keep, axis=-1, keepdims=True)
        return carry, acc

    carry, acc = block(i, jnp.zeros((tq, 1), F32), jnp.zeros((tq, SB_HEAD_DIM), F32), True)

    def cond(state):
        j, carry, _ = state
        return jnp.logical_and(j >= 0, jnp.max(carry) > EXP_UNDERFLOW)

    def body(state):
        j, carry, acc = state
        carry, acc = block(j, carry, acc, False)
        return j - 1, carry, acc

    _, _, acc = lax.while_loop(cond, body, (i - 1, carry, acc))
    o_ref[0] = (acc * _silu(g_ref[0])).astype(o_ref.dtype)


def _attention(qkv, gu, tq):
    b, s, _ = qkv.shape
    kernel = functools.partial(_attn_kernel, tq=tq, scale=SB_HEAD_DIM ** -0.5)
    return pl.pallas_call(
        kernel,
        grid=(b, SB_HEADS, s // tq),
        in_specs=[pl.BlockSpec((1, tq, SB_HEAD_DIM), lambda bi, h, i: (bi, i, h)),
                  pl.BlockSpec((1, s, SB_HEAD_DIM), lambda bi, h, i: (bi, 0, SB_HEADS + h)),
                  pl.BlockSpec((1, s, SB_HEAD_DIM), lambda bi, h, i: (bi, 0, 2 * SB_HEADS + h)),
                  pl.BlockSpec((1, tq, SB_HEAD_DIM), lambda bi, h, i: (bi, i, h))],
        out_specs=pl.BlockSpec((1, tq, SB_HEAD_DIM), lambda bi, h, i: (bi, i, h)),
        out_shape=jax.ShapeDtypeStruct((b, s, SB_WIDTH), BF16),
        compiler_params=_params("parallel", "parallel", "arbitrary"),
        name="stickbreak_attn",
    )(qkv, qkv, qkv, gu)


def _even_tail_kernel(ma_ref, u_ref, halo_ref, gb_ref, x_ref, wp_ref, ps_ref, wo_ref,
                      lg_ref, lb_ref, o_ref, *, tm, seq):
    i = pl.program_id(0)
    first = (i * tm) % seq == 0
    u = u_ref[...]
    halo = jnp.where(first, 0.0, halo_ref[...])
    ext = jnp.concatenate([halo, u], axis=0)
    pos = (i * tm) % seq + lax.broadcasted_iota(jnp.int32, (tm, 1), 0)

    y = jnp.dot(ma_ref[...], wo_ref[0:SB_WIDTH, :], preferred_element_type=F32)
    for g, window in enumerate(POOL_WINDOWS):
        lo = g * POOL_GROUP_DIM
        hi = lo + POOL_GROUP_DIM
        acc = ext[:, lo:hi]
        span = 1
        while span < window:
            acc = acc + pltpu.roll(acc, span, 0)
            span *= 2
        count = jnp.minimum(pos + 1, window).astype(F32)
        pooled = acc[POOL_HALO:, :] / count - u[:, lo:hi]
        mixed = _dot(pooled, wp_ref[g]) * ps_ref[:, lo:hi]
        mb = mixed * _silu(gb_ref[:, lo:hi])
        y = y + _dot(mb, wo_ref[SB_WIDTH + lo:SB_WIDTH + hi, :])

    z = DEEPNORM_ALPHA * x_ref[...] + y
    o_ref[...] = _layer_norm_rows(z, lg_ref[...], lb_ref[...])


def _even_tail(ma, gu, x, w_pool, pool_scale, w_out, ln_g, ln_b, seq, tm):
    m = x.shape[0]
    halo_blocks = tm // POOL_HALO
    u_col = SB_WIDTH // POOL_WIDTH
    kernel = functools.partial(_even_tail_kernel, tm=tm, seq=seq)
    return pl.pallas_call(
        kernel,
        grid=(m // tm,),
        in_specs=[pl.BlockSpec((tm, SB_WIDTH), lambda i: (i, 0)),
                  pl.BlockSpec((tm, POOL_WIDTH), lambda i: (i, u_col)),
                  pl.BlockSpec((POOL_HALO, POOL_WIDTH),
                               lambda i: (jnp.maximum(i * halo_blocks - 1, 0), u_col)),
                  pl.BlockSpec((tm, POOL_WIDTH), lambda i: (i, u_col + 1)),
                  pl.BlockSpec((tm, D_MODEL), lambda i: (i, 0)),
                  pl.BlockSpec(w_pool.shape, lambda i: (0, 0, 0)),
                  pl.BlockSpec((1, POOL_WIDTH), lambda i: (0, 0)),
                  pl.BlockSpec(w_out.shape, lambda i: (0, 0)),
                  pl.BlockSpec((1, D_MODEL), lambda i: (0, 0)),
                  pl.BlockSpec((1, D_MODEL), lambda i: (0, 0))],
        out_specs=pl.BlockSpec((tm, D_MODEL), lambda i: (i, 0)),
        out_shape=jax.ShapeDtypeStruct((m, D_MODEL), F32),
        compiler_params=_params("parallel"),
        name="even_tail",
    )(ma, gu, gu, gu, x, w_pool, pool_scale, w_out, ln_g, ln_b)


def _shifted_delta(x, halo_ref, first):
    last = jnp.where(first, 0.0, halo_ref[SUBLANES - 1:SUBLANES, :])
    row = lax.broadcasted_iota(jnp.int32, (x.shape[0], 1), 0)
    prev = jnp.where(row == 0, last, pltpu.roll(x, 1, 0))
    return prev - x


def _shift_proj_kernel(x_ref, halo_ref, mu_ref, w_ref, o_ref, *, tm, seq):
    i = pl.program_id(1)
    x = x_ref[...]
    xx = _shifted_delta(x, halo_ref, (i * tm) % seq == 0)
    o_ref[0] = _dot(x + xx * mu_ref[0], w_ref[0])


def _shift_project(x, mu4, w4, seq, tm):
    m = x.shape[0]
    n_proj = w4.shape[0]
    halo_blocks = tm // SUBLANES
    kernel = functools.partial(_shift_proj_kernel, tm=tm, seq=seq)
    return pl.pallas_call(
        kernel,
        grid=(n_proj, m // tm),
        in_specs=[pl.BlockSpec((tm, D_MODEL), lambda j, i: (i, 0)),
                  pl.BlockSpec((SUBLANES, D_MODEL),
                               lambda j, i: (jnp.maximum(i * halo_blocks - 1, 0), 0)),
                  pl.BlockSpec((1, 1, D_MODEL), lambda j, i: (j, 0, 0)),
                  pl.BlockSpec((1, D_MODEL, D_MODEL), lambda j, i: (j, 0, 0))],
        out_specs=pl.BlockSpec((1, tm, D_MODEL), lambda j, i: (j, i, 0)),
        out_shape=jax.ShapeDtypeStruct((n_proj, m, D_MODEL), F32),
        compiler_params=_params("parallel", "parallel"),
        name="shift_proj",
    )(x, x, mu4, w4)


def _lora_kernel(x_ref, halo_ref, mu_ref, w1_ref, w2_ref, w0_ref, a1_ref, a2_ref, a0_ref,
                 lw_ref, a_ref, *, tm, seq):
    i = pl.program_id(0)
    x = x_ref[...]
    xx = _shifted_delta(x, halo_ref, (i * tm) % seq == 0)
    xw = x + xx * mu_ref[0:1, :]
    xa = x + xx * mu_ref[1:2, :]
    ww = w0_ref[...] + _dot(jnp.tanh(_dot(xw, w1_ref[...])), w2_ref[...])
    w_log = -_softplus(-ww) - 0.5
    lw_ref[...] = -jnp.exp(w_log)
    aa = a0_ref[...] + _dot(_dot(xa, a1_ref[...]), a2_ref[...])
    a_ref[...] = 1.0 / (1.0 + jnp.exp(-aa))


def _lora(x, mu2, w1, w2, w0, a1, a2, a0, seq, tm):
    m = x.shape[0]
    halo_blocks = tm // SUBLANES
    kernel = functools.partial(_lora_kernel, tm=tm, seq=seq)
    full = lambda arr: pl.BlockSpec(arr.shape, lambda i: (0,) * arr.ndim)
    out = jax.ShapeDtypeStruct((m, D_MODEL), F32)
    return pl.pallas_call(
        kernel,
        grid=(m // tm,),
        in_specs=[pl.BlockSpec((tm, D_MODEL), lambda i: (i, 0)),
                  pl.BlockSpec((SUBLANES, D_MODEL),
                               lambda i: (jnp.maximum(i * halo_blocks - 1, 0), 0)),
                  full(mu2), full(w1), full(w2), full(w0), full(a1), full(a2), full(a0)],
        out_specs=[pl.BlockSpec((tm, D_MODEL), lambda i: (i, 0)),
                   pl.BlockSpec((tm, D_MODEL), lambda i: (i, 0))],
        out_shape=[out, out],
        compiler_params=_params("parallel"),
        name="decay_iclr_lora",
    )(x, x, mu2, w1, w2, w0, a1, a2, a0)


def _stack_heads(x, head0):
    return jnp.concatenate([jnp.where(head0, x, 0.0), jnp.where(head0, 0.0, x)], axis=0)


def _head_sum(x, head0):
    s0 = jnp.sum(jnp.where(head0, x, 0.0), axis=-1, keepdims=True)
    s1 = jnp.sum(jnp.where(head0, 0.0, x), axis=-1, keepdims=True)
    return jnp.where(head0, s0, s1)


def _wkv_kernel(r_ref, k_ref, v_ref, g_ref, lw_ref, a_ref, kk_ref, ka_ref, rk_ref,
                gw_ref, gb_ref, o_ref, state_ref, *, chunks):
    @pl.when(pl.program_id(2) == 0)
    def _():
        state_ref[...] = jnp.zeros_like(state_ref)

    c = WKV_CHUNK
    p = WKV_PAIR
    lane = lax.broadcasted_iota(jnp.int32, (1, LANES), 1)
    head0 = lane < RWKV_HEAD_DIM
    prow = lax.broadcasted_iota(jnp.int32, (p, p), 0)
    pcol = lax.broadcasted_iota(jnp.int32, (p, p), 1)
    same_head = (prow // c) == (pcol // c)
    strict = jnp.logical_and(same_head, (prow % c) > (pcol % c))
    lower = jnp.logical_and(same_head, (prow % c) >= (pcol % c))
    eye = jnp.where(prow == pcol, 1.0, 0.0)
    crow = lax.broadcasted_iota(jnp.int32, (c, c), 0)
    ccol = lax.broadcasted_iota(jnp.int32, (c, c), 1)
    prefix = jnp.where(crow >= ccol, 1.0, 0.0).astype(BF16)

    state = state_ref[...]
    for ci in range(chunks):
        rows = pl.ds(ci * c, c)
        r = r_ref[0, 0, rows, :]
        k = k_ref[0, 0, rows, :]
        v = v_ref[0, 0, rows, :]
        lw = lw_ref[0, rows, :]
        a = a_ref[0, rows, :]

        kk = k * kk_ref[...]
        norm = jnp.sqrt(_head_sum(kk * kk, head0))
        kk = kk / jnp.maximum(norm, L2_EPS)
        kmod = k * (1.0 + (a - 1.0) * ka_ref[...])
        b = a * kk

        lw_hi = lw.astype(BF16)
        lw_lo = (lw - lw_hi.astype(F32)).astype(BF16)
        cum = (jnp.dot(prefix, lw_hi, preferred_element_type=F32)
               + jnp.dot(prefix, lw_lo, preferred_element_type=F32))
        total = cum[c - 1:c, :]
        g_in = jnp.exp(cum)
        g_ex = jnp.exp(cum - lw)
        g_inv = jnp.exp(-cum)
        g_tail = jnp.exp(total - cum)

        kkg = _stack_heads(kk * g_ex, head0)
        rg = _stack_heads(r * g_in, head0)
        kd = _stack_heads(kmod * g_inv, head0)
        bd = _stack_heads(b * g_inv, head0)
        kt = _stack_heads(kmod * g_tail, head0)
        bt = _stack_heads(b * g_tail, head0)
        vs = _stack_heads(v, head0)

        a_kb = jnp.where(strict, _dot_nt(kkg, bd), 0.0)
        a_kk = jnp.where(strict, _dot_nt(kkg, kd), 0.0)
        a_rk = jnp.where(lower, _dot_nt(rg, kd), 0.0)
        a_rb = jnp.where(lower, _dot_nt(rg, bd), 0.0)

        inv = eye - a_kb
        power = a_kb
        span = 2
        while span < c + 1:
            power = _dot(power, power)
            inv = inv + _dot(inv, power)
            span *= 2

        w1 = _dot(inv, kkg)
        u0 = _dot(inv, _dot(a_kk, vs))
        r2 = rg - _dot(a_rb, w1)
        o0 = _dot(a_rk, vs) - _dot(a_rb, u0)
        decay_diag = jnp.where(prow == pcol, jnp.exp(total), 0.0)
        trans = decay_diag - _dot_tn(bt, w1)
        fresh = _dot_tn(kt, vs) - _dot_tn(bt, u0)

        o_stack = _dot(r2, state) + o0
        state = _dot(trans, state) + fresh
        o = o_stack[0:c, :] + o_stack[c:p, :]

        mean = _head_sum(o, head0) * (1.0 / RWKV_HEAD_DIM)
        d = o - mean
        var = _head_sum(d * d, head0) * (1.0 / RWKV_HEAD_DIM)
        normed = d * lax.rsqrt(var + GN_EPS) * gw_ref[...] + gb_ref[...]
        bonus = _head_sum(r * kmod * rk_ref[...], head0) * v
        o_ref[0, rows, :] = ((normed + bonus) * _silu(g_ref[0, 0, rows, :])).astype(o_ref.dtype)

    state_ref[...] = state


def _wkv(rkvg, lw, a, k_k, k_a, r_k, gn_w, gn_b, chunks):
    _, b, s, d = rkvg.shape
    t = chunks * WKV_CHUNK
    kernel = functools.partial(_wkv_kernel, chunks=chunks)
    proj = lambda j: pl.BlockSpec((1, 1, t, LANES), lambda bi, h, ci: (j, bi, ci, h))
    tok = pl.BlockSpec((1, t, LANES), lambda bi, h, ci: (bi, ci, h))
    vec = pl.BlockSpec((1, LANES), lambda bi, h, ci: (0, h))
    return pl.pallas_call(
        kernel,
        grid=(b, d // LANES, s // t),
        in_specs=[proj(0), proj(1), proj(2), proj(3), tok, tok, vec, vec, vec, vec, vec],
        out_specs=tok,
        out_shape=jax.ShapeDtypeStruct((b, s, d), BF16),
        scratch_shapes=[pltpu.VMEM((LANES, LANES), F32)],
        compiler_params=_params("parallel", "parallel", "arbitrary"),
        name="wkv7_chunked",
    )(rkvg, rkvg, rkvg, rkvg, lw, a, k_k, k_a, r_k, gn_w, gn_b)


def _odd_tail_kernel(m_ref, x_ref, wo_ref, lg_ref, lb_ref, o_ref):
    y = jnp.dot(m_ref[...], wo_ref[...], preferred_element_type=F32)
    z = DEEPNORM_ALPHA * x_ref[...] + y
    o_ref[...] = _layer_norm_rows(z, lg_ref[...], lb_ref[...])


def _odd_tail(mixed, x, w_o, ln_g, ln_b, tm):
    m = x.shape[0]
    return pl.pallas_call(
        _odd_tail_kernel,
        grid=(m // tm,),
        in_specs=[pl.BlockSpec((tm, D_MODEL), lambda i: (i, 0)),
                  pl.BlockSpec((tm, D_MODEL), lambda i: (i, 0)),
                  pl.BlockSpec(w_o.shape, lambda i: (0, 0)),
                  pl.BlockSpec((1, D_MODEL), lambda i: (0, 0)),
                  pl.BlockSpec((1, D_MODEL), lambda i: (0, 0))],
        out_specs=pl.BlockSpec((tm, D_MODEL), lambda i: (i, 0)),
        out_shape=jax.ShapeDtypeStruct((m, D_MODEL), F32),
        compiler_params=_params("parallel"),
        name="odd_tail",
    )(mixed, x, w_o, ln_g, ln_b)


def _even_layer(x2, seq, w_in, w_pool, pool_scale, w_out, ln_g, ln_b):
    m = x2.shape[0]
    batch = m // seq
    split = 3 * SB_WIDTH
    w_in = w_in.astype(BF16)
    qkv = _project(x2, w_in[:, :split], BF16, tm=512, tn=512)
    gu = _project(x2, w_in[:, split:], F32, tm=512, tn=512)
    ma = _attention(qkv.reshape(batch, seq, split), gu.reshape(batch, seq, split), tq=128)
    return _even_tail(ma.reshape(m, SB_WIDTH), gu, x2, w_pool.astype(BF16),
                      pool_scale.reshape(1, POOL_WIDTH), w_out.astype(BF16),
                      ln_g.reshape(1, D_MODEL), ln_b.reshape(1, D_MODEL), seq, tm=256)


def _pad_rank(w, axis):
    pad = [(0, 0), (0, 0)]
    pad[axis] = (0, LORA_PAD - w.shape[axis])
    return jnp.pad(w, pad).astype(BF16)


def _odd_layer(x2, seq, mu, w_r, w_k, w_v, w_g, w0, w1, w2, a0, a1, a2, k_k, k_a, r_k,
               gn_w, gn_b, w_o, ln_g, ln_b):
    m = x2.shape[0]
    batch = m // seq
    row = lambda vec: vec.reshape(1, D_MODEL)
    mu4 = jnp.stack([mu[0], mu[2], mu[3], mu[5]]).reshape(4, 1, D_MODEL)
    w4 = jnp.stack([w_r, w_k, w_v, w_g]).astype(BF16)
    rkvg = _shift_project(x2, mu4, w4, seq, tm=256)
    mu2 = jnp.stack([mu[1], mu[4]])
    lw, a = _lora(x2, mu2, _pad_rank(w1, 1), _pad_rank(w2, 0), row(w0),
                  _pad_rank(a1, 1), _pad_rank(a2, 0), row(a0), seq, tm=256)
    mixed = _wkv(rkvg.reshape(4, batch, seq, D_MODEL), lw.reshape(batch, seq, D_MODEL),
                 a.reshape(batch, seq, D_MODEL), row(k_k), row(k_a), row(r_k),
                 row(gn_w), row(gn_b), chunks=4)
    return _odd_tail(mixed.reshape(m, D_MODEL), x2, w_o.astype(BF16), row(ln_g), row(ln_b), tm=256)


def kernel(x, ev_w_in, ev_w_pool, ev_pool_scale, ev_w_out, od_mu, od_w_r, od_w_k, od_w_v, od_w_g,
           od_w0, od_w1, od_w2, od_a0, od_a1, od_a2, od_k_k, od_k_a, od_r_k, od_gn_w, od_gn_b,
           od_w_o, ln_g, ln_b):
    batch, seq, d = x.shape
    x2 = x.reshape(batch * seq, d)
    for layer in range(N_LAYERS):
        j = layer // 2
        if layer % 2 == 0:
            x2 = _even_layer(x2, seq, ev_w_in[j], ev_w_pool[j], ev_pool_scale[j], ev_w_out[j],
                             ln_g[layer], ln_b[layer])
        else:
            x2 = _odd_layer(x2, seq, od_mu[j], od_w_r[j], od_w_k[j], od_w_v[j], od_w_g[j],
                            od_w0[j], od_w1[j], od_w2[j], od_a0[j], od_a1[j], od_a2[j],
                            od_k_k[j], od_k_a[j], od_r_k[j], od_gn_w[j], od_gn_b[j], od_w_o[j],
                            ln_g[layer], ln_b[layer])
    return x2.reshape(batch, seq, d)
```

```python
import functools

import jax
import jax.numpy as jnp
from jax import lax
from jax.experimental import pallas as pl
from jax.experimental.pallas import tpu as pltpu

F32 = jnp.float32
BF16 = jnp.bfloat16

D_MODEL = 2048
N_LAYERS = 2
SB_HEADS = 8
SB_HEAD_DIM = 128
SB_WIDTH = SB_HEADS * SB_HEAD_DIM
POOL_WINDOWS = (2, 4, 8, 16)
POOL_GROUP_DIM = 256
POOL_WIDTH = len(POOL_WINDOWS) * POOL_GROUP_DIM
POOL_HALO = 16
RWKV_HEAD_DIM = 64
LORA_PAD = 128
GN_EPS = 64e-5
L2_EPS = 1e-12
LN_EPS = 1e-5
DEEPNORM_ALPHA = (2 * N_LAYERS) ** 0.25

SUBLANES = 8
LANES = 128
VMEM_LIMIT = 52 * 1024 * 1024

EXP_UNDERFLOW = -104.0

WKV_CHUNK = 64
WKV_HEADS = 4
WKV_LANES = WKV_HEADS * RWKV_HEAD_DIM
assert WKV_CHUNK == RWKV_HEAD_DIM


def _dot(a, b):
    return jnp.dot(a.astype(BF16), b.astype(BF16), preferred_element_type=F32)


def _dot_nt(a, b):
    return lax.dot_general(a.astype(BF16), b.astype(BF16), (((1,), (1,)), ((), ())),
                           preferred_element_type=F32)


def _dot_tn(a, b):
    return lax.dot_general(a.astype(BF16), b.astype(BF16), (((0,), (0,)), ((), ())),
                           preferred_element_type=F32)


def _silu(g):
    return g / (1.0 + jnp.exp(-g))


def _softplus(x):
    return jnp.maximum(x, 0.0) + jnp.log1p(jnp.exp(-jnp.abs(x)))


def _layer_norm_rows(z, g, b):
    mean = jnp.mean(z, axis=-1, keepdims=True)
    d = z - mean
    var = jnp.mean(d * d, axis=-1, keepdims=True)
    return d * lax.rsqrt(var + LN_EPS) * g + b


def _params(*sem):
    return pltpu.CompilerParams(dimension_semantics=sem, vmem_limit_bytes=VMEM_LIMIT)


def _proj_kernel(x_ref, w_ref, o_ref, xb_ref):
    @pl.when(pl.program_id(1) == 0)
    def _():
        xb_ref[...] = x_ref[...].astype(BF16)

    o_ref[...] = jnp.dot(xb_ref[...], w_ref[...], preferred_element_type=F32).astype(o_ref.dtype)


def _project(x, w, out_dtype, tm, tn):
    m, k = x.shape
    n = w.shape[1]
    return pl.pallas_call(
        _proj_kernel,
        grid=(m // tm, n // tn),
        in_specs=[pl.BlockSpec((tm, k), lambda i, j: (i, 0)),
                  pl.BlockSpec((k, tn), lambda i, j: (0, j))],
        out_specs=pl.BlockSpec((tm, tn), lambda i, j: (i, j)),
        out_shape=jax.ShapeDtypeStruct((m, n), out_dtype),
        scratch_shapes=[pltpu.VMEM((tm, k), BF16)],
        compiler_params=_params("parallel", "arbitrary"),
        name="in_proj",
    )(x, w)


def _attn_kernel(q_ref, k_ref, v_ref, g_ref, o_ref, *, tq, tk, scale):
    i = pl.program_id(2)
    inner = tq // tk
    krow = lax.broadcasted_iota(jnp.int32, (tk, tk), 0)
    kcol = lax.broadcasted_iota(jnp.int32, (tk, tk), 1)
    after = jnp.where(krow > kcol, 1.0, 0.0).astype(BF16)

    def block(q, start, carry, acc, diagonal):
        kb = k_ref[0, pl.ds(start, tk), :]
        vb = v_ref[0, pl.ds(start, tk), :]
        z = _dot_nt(q, kb) * scale
        t = jnp.log1p(jnp.exp(-jnp.abs(z)))
        log_keep = -(jnp.maximum(z, 0.0) + t)
        log_beta = jnp.minimum(z, 0.0) - t
        if diagonal:
            causal = (lax.broadcasted_iota(jnp.int32, z.shape, 1)
                      < lax.broadcasted_iota(jnp.int32, z.shape, 0))
            log_keep = jnp.where(causal, log_keep, 0.0)
        hi = log_keep.astype(BF16)
        lo = (log_keep - hi.astype(F32)).astype(BF16)
        later = (jnp.dot(hi, after, preferred_element_type=F32)
                 + jnp.dot(lo, after, preferred_element_type=F32) + carry)
        w = jnp.exp(log_beta + later)
        if diagonal:
            w = jnp.where(causal, w, 0.0)
        acc = acc + _dot(w, vb)
        carry = carry + jnp.sum(log_keep, axis=-1, keepdims=True)
        return carry, acc

    carry = jnp.zeros((tq, 1), F32)
    acc = jnp.zeros((tq, SB_HEAD_DIM), F32)
    for jo in reversed(range(inner)):
        lo_row = jo * tk
        start = pl.multiple_of(i * tq + lo_row, tk)
        c_new, a_new = block(q_ref[0, lo_row:tq, :], start, carry[lo_row:tq], acc[lo_row:tq], True)
        carry = jnp.concatenate([carry[0:lo_row], c_new], axis=0) if lo_row else c_new
        acc = jnp.concatenate([acc[0:lo_row], a_new], axis=0) if lo_row else a_new

    def cond(state):
        j, carry, _ = state
        return jnp.logical_and(j >= 0, jnp.max(carry) > EXP_UNDERFLOW)

    def body(state):
        j, carry, acc = state
        carry, acc = block(q_ref[0], pl.multiple_of(j * tk, tk), carry, acc, False)
        return j - 1, carry, acc

    _, _, acc = lax.while_loop(cond, body, (i * inner - 1, carry, acc))
    o_ref[0] = (acc * _silu(g_ref[0])).astype(o_ref.dtype)


def _attention(qkv, gu, tq, tk):
    b, s, _ = qkv.shape
    kernel = functools.partial(_attn_kernel, tq=tq, tk=tk, scale=SB_HEAD_DIM ** -0.5)
    return pl.pallas_call(
        kernel,
        grid=(b, SB_HEADS, s // tq),
        in_specs=[pl.BlockSpec((1, tq, SB_HEAD_DIM), lambda bi, h, i: (bi, i, h)),
                  pl.BlockSpec((1, s, SB_HEAD_DIM), lambda bi, h, i: (bi, 0, SB_HEADS + h)),
                  pl.BlockSpec((1, s, SB_HEAD_DIM), lambda bi, h, i: (bi, 0, 2 * SB_HEADS + h)),
                  pl.BlockSpec((1, tq, SB_HEAD_DIM), lambda bi, h, i: (bi, i, h))],
        out_specs=pl.BlockSpec((1, tq, SB_HEAD_DIM), lambda bi, h, i: (bi, i, h)),
        out_shape=jax.ShapeDtypeStruct((b, s, SB_WIDTH), BF16),
        compiler_params=_params("parallel", "parallel", "arbitrary"),
        name="stickbreak_attn",
    )(qkv, qkv, qkv, gu)


def _even_tail_kernel(ma_ref, u_ref, halo_ref, gb_ref, x_ref, wp_ref, ps_ref, wo_ref,
                      lg_ref, lb_ref, o_ref, *, tm, seq):
    i = pl.program_id(0)
    first = (i * tm) % seq == 0
    u = u_ref[...]
    halo = jnp.where(first, 0.0, halo_ref[...])
    ext = jnp.concatenate([halo, u], axis=0)
    pos = (i * tm) % seq + lax.broadcasted_iota(jnp.int32, (tm, 1), 0)

    y = jnp.dot(ma_ref[...], wo_ref[0:SB_WIDTH, :], preferred_element_type=F32)
    for g, window in enumerate(POOL_WINDOWS):
        lo = g * POOL_GROUP_DIM
        hi = lo + POOL_GROUP_DIM
        acc = ext[:, lo:hi]
        span = 1
        while span < window:
            acc = acc + pltpu.roll(acc, span, 0)
            span *= 2
        count = jnp.minimum(pos + 1, window).astype(F32)
        pooled = acc[POOL_HALO:, :] / count - u[:, lo:hi]
        mixed = _dot(pooled, wp_ref[g]) * ps_ref[:, lo:hi]
        mb = mixed * _silu(gb_ref[:, lo:hi])
        y = y + _dot(mb, wo_ref[SB_WIDTH + lo:SB_WIDTH + hi, :])

    z = DEEPNORM_ALPHA * x_ref[...] + y
    o_ref[...] = _layer_norm_rows(z, lg_ref[...], lb_ref[...])


def _even_tail(ma, gu, x, w_pool, pool_scale, w_out, ln_g, ln_b, seq, tm):
    m = x.shape[0]
    halo_blocks = tm // POOL_HALO
    u_col = SB_WIDTH // POOL_WIDTH
    kernel = functools.partial(_even_tail_kernel, tm=tm, seq=seq)
    return pl.pallas_call(
        kernel,
        grid=(m // tm,),
        in_specs=[pl.BlockSpec((tm, SB_WIDTH), lambda i: (i, 0)),
                  pl.BlockSpec((tm, POOL_WIDTH), lambda i: (i, u_col)),
                  pl.BlockSpec((POOL_HALO, POOL_WIDTH),
                               lambda i: (jnp.maximum(i * halo_blocks - 1, 0), u_col)),
                  pl.BlockSpec((tm, POOL_WIDTH), lambda i: (i, u_col + 1)),
                  pl.BlockSpec((tm, D_MODEL), lambda i: (i, 0)),
                  pl.BlockSpec(w_pool.shape, lambda i: (0, 0, 0)),
                  pl.BlockSpec((1, POOL_WIDTH), lambda i: (0, 0)),
                  pl.BlockSpec(w_out.shape, lambda i: (0, 0)),
                  pl.BlockSpec((1, D_MODEL), lambda i: (0, 0)),
                  pl.BlockSpec((1, D_MODEL), lambda i: (0, 0))],
        out_specs=pl.BlockSpec((tm, D_MODEL), lambda i: (i, 0)),
        out_shape=jax.ShapeDtypeStruct((m, D_MODEL), F32),
        compiler_params=_params("parallel"),
        name="even_tail",
    )(ma, gu, gu, gu, x, w_pool, pool_scale, w_out, ln_g, ln_b)


def _shifted_delta(x, halo_ref, first):
    last = jnp.where(first, 0.0, halo_ref[SUBLANES - 1:SUBLANES, :])
    row = lax.broadcasted_iota(jnp.int32, (x.shape[0], 1), 0)
    prev = jnp.where(row == 0, last, pltpu.roll(x, 1, 0))
    return prev - x


def _shift_proj_kernel(x_ref, halo_ref, mu_ref, w_ref, o_ref, *, tm, seq):
    i = pl.program_id(1)
    x = x_ref[...]
    xx = _shifted_delta(x, halo_ref, (i * tm) % seq == 0)
    o_ref[0] = _dot(x + xx * mu_ref[0], w_ref[0])


def _shift_project(x, mu4, w4, seq, tm):
    m = x.shape[0]
    n_proj = w4.shape[0]
    halo_blocks = tm // SUBLANES
    kernel = functools.partial(_shift_proj_kernel, tm=tm, seq=seq)
    return pl.pallas_call(
        kernel,
        grid=(n_proj, m // tm),
        in_specs=[pl.BlockSpec((tm, D_MODEL), lambda j, i: (i, 0)),
                  pl.BlockSpec((SUBLANES, D_MODEL),
                               lambda j, i: (jnp.maximum(i * halo_blocks - 1, 0), 0)),
                  pl.BlockSpec((1, 1, D_MODEL), lambda j, i: (j, 0, 0)),
                  pl.BlockSpec((1, D_MODEL, D_MODEL), lambda j, i: (j, 0, 0))],
        out_specs=pl.BlockSpec((1, tm, D_MODEL), lambda j, i: (j, i, 0)),
        out_shape=jax.ShapeDtypeStruct((n_proj, m, D_MODEL), F32),
        compiler_params=_params("parallel", "parallel"),
        name="shift_proj",
    )(x, x, mu4, w4)


def _lora_kernel(x_ref, halo_ref, mu_ref, w1_ref, w2_ref, w0_ref, a1_ref, a2_ref, a0_ref,
                 lw_ref, a_ref, *, tm, seq):
    i = pl.program_id(0)
    x = x_ref[...]
    xx = _shifted_delta(x, halo_ref, (i * tm) % seq == 0)
    xw = x + xx * mu_ref[0:1, :]
    xa = x + xx * mu_ref[1:2, :]
    ww = w0_ref[...] + _dot(jnp.tanh(_dot(xw, w1_ref[...])), w2_ref[...])
    w_log = -_softplus(-ww) - 0.5
    lw_ref[...] = -jnp.exp(w_log)
    aa = a0_ref[...] + _dot(_dot(xa, a1_ref[...]), a2_ref[...])
    a_ref[...] = 1.0 / (1.0 + jnp.exp(-aa))


def _lora(x, mu2, w1, w2, w0, a1, a2, a0, seq, tm):
    m = x.shape[0]
    halo_blocks = tm // SUBLANES
    kernel = functools.partial(_lora_kernel, tm=tm, seq=seq)
    full = lambda arr: pl.BlockSpec(arr.shape, lambda i: (0,) * arr.ndim)
    out = jax.ShapeDtypeStruct((m, D_MODEL), F32)
    return pl.pallas_call(
        kernel,
        grid=(m // tm,),
        in_specs=[pl.BlockSpec((tm, D_MODEL), lambda i: (i, 0)),
                  pl.BlockSpec((SUBLANES, D_MODEL),
                               lambda i: (jnp.maximum(i * halo_blocks - 1, 0), 0)),
                  full(mu2), full(w1), full(w2), full(w0), full(a1), full(a2), full(a0)],
        out_specs=[pl.BlockSpec((tm, D_MODEL), lambda i: (i, 0)),
                   pl.BlockSpec((tm, D_MODEL), lambda i: (i, 0))],
        out_shape=[out, out],
        compiler_params=_params("parallel"),
        name="decay_iclr_lora",
    )(x, x, mu2, w1, w2, w0, a1, a2, a0)


def _blockdiag(y, bd_mask):
    tiled = jnp.concatenate([y.astype(BF16)] * WKV_HEADS, axis=0)
    return tiled * bd_mask


def _fold_diag_blocks(full, head_masks):
    c = WKV_CHUNK
    out = jnp.where(head_masks[0], full[0:c, :], 0.0)
    for h in range(1, WKV_HEADS):
        out = out + jnp.where(head_masks[h], full[h * c:(h + 1) * c, :], 0.0)
    return out


def _wkv_kernel(r_ref, k_ref, v_ref, g_ref, lw_ref, a_ref, kk_ref, ka_ref, rk_ref,
                gw_ref, gb_ref, o_ref, state_ref, *, chunks):
    @pl.when(pl.program_id(2) == 0)
    def _():
        state_ref[...] = jnp.zeros_like(state_ref)

    c = WKV_CHUNK
    w = WKV_LANES
    t = chunks * c
    srow = lax.broadcasted_iota(jnp.int32, (w, w), 0)
    scol = lax.broadcasted_iota(jnp.int32, (w, w), 1)
    bd_mask = jnp.where((srow // c) == (scol // c), 1.0, 0.0).astype(BF16)
    lane = lax.broadcasted_iota(jnp.int32, (1, w), 1)
    head_masks = [(lane // c) == h for h in range(WKV_HEADS)]
    crow = lax.broadcasted_iota(jnp.int32, (c, w), 0)
    ccol = lax.broadcasted_iota(jnp.int32, (c, w), 1) % c
    strict = crow > ccol
    lower = crow >= ccol
    eye_cat = jnp.where(crow == ccol, 1.0, 0.0)
    tg = min(t, w)
    trow = lax.broadcasted_iota(jnp.int32, (tg, tg), 0)
    tcol = lax.broadcasted_iota(jnp.int32, (tg, tg), 1)
    prefix = jnp.where(jnp.logical_and((trow // c) == (tcol // c), trow >= tcol),
                       1.0, 0.0).astype(BF16)

    r = r_ref[0, 0]
    k = k_ref[0, 0]
    v = v_ref[0, 0]
    lw = lw_ref[0]
    a = a_ref[0]

    kk = k * kk_ref[...]
    kmod = k * (1.0 + (a - 1.0) * ka_ref[...])
    sums = _dot(jnp.concatenate([kk * kk, r * kmod * rk_ref[...]], axis=0), bd_mask)
    kk = kk / jnp.maximum(jnp.sqrt(sums[0:t]), L2_EPS)
    bonus = sums[t:2 * t] * v
    b = a * kk

    lw_hi = lw.astype(BF16)
    lw_lo = (lw - lw_hi.astype(F32)).astype(BF16)
    lw_split = jnp.concatenate([lw_hi, lw_lo], axis=1)
    cum2 = jnp.concatenate([jnp.dot(prefix, lw_split[s:s + tg], preferred_element_type=F32)
                            for s in range(0, t, tg)], axis=0)
    cum = cum2[:, 0:w] + cum2[:, w:2 * w]
    totals = [cum[(ci + 1) * c - 1:(ci + 1) * c, :] for ci in range(chunks)]
    total_rows = jnp.concatenate([jnp.broadcast_to(tot, (c, w)) for tot in totals], axis=0)
    g_in = jnp.exp(cum)
    g_inv = jnp.exp(-cum)
    kkg = kk * jnp.exp(cum - lw)
    rg = r * g_in
    kd = kmod * g_inv
    bd = b * g_inv
    g_tail = jnp.exp(total_rows - cum)
    kt = kmod * g_tail
    bt = b * g_tail

    every = range(chunks)
    sl = lambda x, ci: x[ci * c:(ci + 1) * c]
    bdm = lambda y: _blockdiag(y, bd_mask)

    lhs = [jnp.concatenate([sl(kkg, ci), sl(rg, ci)], axis=0) for ci in every]
    a1 = [_dot_nt(lhs[ci], bdm(sl(bd, ci))) for ci in every]
    a2 = [_dot_nt(lhs[ci], bdm(sl(kd, ci))) for ci in every]
    a_kb = [jnp.where(strict, a1[ci][0:c], 0.0) for ci in every]
    a_rb = [jnp.where(lower, a1[ci][c:2 * c], 0.0) for ci in every]
    a_kk = [jnp.where(strict, a2[ci][0:c], 0.0) for ci in every]
    a_rk = [jnp.where(lower, a2[ci][c:2 * c], 0.0) for ci in every]

    power = [-a_kb[ci] for ci in every]
    inv = [eye_cat + power[ci] for ci in every]
    power = [_dot(power[ci], bdm(power[ci])) for ci in every]
    span = 2
    while 2 * span < c:
        both = [_dot(jnp.concatenate([inv[ci], power[ci]], axis=0), bdm(power[ci])) for ci in every]
        inv = [inv[ci] + both[ci][0:c] for ci in every]
        power = [both[ci][c:2 * c] for ci in every]
        span *= 2
    inv = [inv[ci] + _dot(inv[ci], bdm(power[ci])) for ci in every]

    av = [_dot(jnp.concatenate([a_kk[ci], a_rk[ci]], axis=0), bdm(sl(v, ci))) for ci in every]
    akkv = [av[ci][0:c] for ci in every]
    arkv = [av[ci][c:2 * c] for ci in every]
    wu =[_dot(inv[ci], jnp.concatenate([bdm(sl(kkg, ci)), bdm(akkv[ci])], axis=1)) for ci in every]
    w1 = [wu[ci][:, 0:w] for ci in every]
    u0 = [wu[ci][:, w:2 * w] for ci in every]
    arb = [_dot(a_rb[ci], jnp.concatenate([bdm(w1[ci]), bdm(u0[ci])], axis=1)) for ci in every]
    r2 = [sl(rg, ci) - arb[ci][:, 0:w] for ci in every]
    o0 = [arkv[ci] - arb[ci][:, w:2 * w] for ci in every]
    trans = [jnp.where(crow == ccol, jnp.exp(totals[ci]), 0.0)
             - _fold_diag_blocks(_dot_tn(sl(bt, ci), w1[ci]), head_masks) for ci in every]
    fresh = [_fold_diag_blocks(
        _dot_tn(jnp.concatenate([sl(kt, ci), -sl(bt, ci)], axis=0),
                jnp.concatenate([sl(v, ci), u0[ci]], axis=0)), head_masks) for ci in every]

    state = state_ref[...]
    outs = []
    for ci in every:
        both = _dot(jnp.concatenate([r2[ci], trans[ci]], axis=0), bdm(state))
        outs.append(both[0:c] + o0[ci])
        state = both[c:2 * c] + fresh[ci]
    state_ref[...] = state
    o = jnp.concatenate(outs, axis=0)

    inv_n = 1.0 / RWKV_HEAD_DIM
    d = o - _dot(o, bd_mask) * inv_n
    var = _dot(d * d, bd_mask) * inv_n
    normed = d * lax.rsqrt(var + GN_EPS) * gw_ref[...] + gb_ref[...]
    o_ref[0] = ((normed + bonus) * _silu(g_ref[0, 0])).astype(o_ref.dtype)


def _wkv(rkvg, lw, a, k_k, k_a, r_k, gn_w, gn_b, chunks):
    _, b, s, d = rkvg.shape
    t = chunks * WKV_CHUNK
    kernel = functools.partial(_wkv_kernel, chunks=chunks)
    proj = lambda j: pl.BlockSpec((1, 1, t, WKV_LANES), lambda bi, h, ci: (j, bi, ci, h))
    tok = pl.BlockSpec((1, t, WKV_LANES), lambda bi, h, ci: (bi, ci, h))
    vec = pl.BlockSpec((1, WKV_LANES), lambda bi, h, ci: (0, h))
    return pl.pallas_call(
        kernel,
        grid=(b, d // WKV_LANES, s // t),
        in_specs=[proj(0), proj(1), proj(2), proj(3), tok, tok, vec, vec, vec, vec, vec],
        out_specs=tok,
        out_shape=jax.ShapeDtypeStruct((b, s, d), BF16),
        scratch_shapes=[pltpu.VMEM((RWKV_HEAD_DIM, WKV_LANES), F32)],
        compiler_params=_params("parallel", "parallel", "arbitrary"),
        name="wkv7_chunked",
    )(rkvg, rkvg, rkvg, rkvg, lw, a, k_k, k_a, r_k, gn_w, gn_b)


def _odd_tail_kernel(m_ref, x_ref, wo_ref, lg_ref, lb_ref, o_ref):
    y = jnp.dot(m_ref[...], wo_ref[...], preferred_element_type=F32)
    z = DEEPNORM_ALPHA * x_ref[...] + y
    o_ref[...] = _layer_norm_rows(z, lg_ref[...], lb_ref[...])


def _odd_tail(mixed, x, w_o, ln_g, ln_b, tm):
    m = x.shape[0]
    return pl.pallas_call(
        _odd_tail_kernel,
        grid=(m // tm,),
        in_specs=[pl.BlockSpec((tm, D_MODEL), lambda i: (i, 0)),
                  pl.BlockSpec((tm, D_MODEL), lambda i: (i, 0)),
                  pl.BlockSpec(w_o.shape, lambda i: (0, 0)),
                  pl.BlockSpec((1, D_MODEL), lambda i: (0, 0)),
                  pl.BlockSpec((1, D_MODEL), lambda i: (0, 0))],
        out_specs=pl.BlockSpec((tm, D_MODEL), lambda i: (i, 0)),
        out_shape=jax.ShapeDtypeStruct((m, D_MODEL), F32),
        compiler_params=_params("parallel"),
        name="odd_tail",
    )(mixed, x, w_o, ln_g, ln_b)


def _even_layer(x2, seq, w_in, w_pool, pool_scale, w_out, ln_g, ln_b):
    m = x2.shape[0]
    batch = m // seq
    split = 3 * SB_WIDTH
    w_in = w_in.astype(BF16)
    qkv = _project(x2, w_in[:, :split], BF16, tm=1024, tn=1024)
    gu = _project(x2, w_in[:, split:], F32, tm=1024, tn=1024)
    ma = _attention(qkv.reshape(batch, seq, split), gu.reshape(batch, seq, split), tq=512, tk=256)
    return _even_tail(ma.reshape(m, SB_WIDTH), gu, x2, w_pool.astype(BF16),
                      pool_scale.reshape(1, POOL_WIDTH), w_out.astype(BF16),
                      ln_g.reshape(1, D_MODEL), ln_b.reshape(1, D_MODEL), seq, tm=256)


def _pad_rank(w, axis):
    pad = [(0, 0), (0, 0)]
    pad[axis] = (0, LORA_PAD - w.shape[axis])
    return jnp.pad(w, pad).astype(BF16)


def _odd_layer(x2, seq, mu, w_r, w_k, w_v, w_g, w0, w1, w2, a0, a1, a2, k_k, k_a, r_k,
               gn_w, gn_b, w_o, ln_g, ln_b):
    m = x2.shape[0]
    batch = m // seq
    row = lambda vec: vec.reshape(1, D_MODEL)
    mu4 = jnp.stack([mu[0], mu[2], mu[3], mu[5]]).reshape(4, 1, D_MODEL)
    w4 = jnp.stack([w_r, w_k, w_v, w_g]).astype(BF16)
    rkvg = _shift_project(x2, mu4, w4, seq, tm=256)
    mu2 = jnp.stack([mu[1], mu[4]])
    lw, a = _lora(x2, mu2, _pad_rank(w1, 1), _pad_rank(w2, 0), row(w0),
                  _pad_rank(a1, 1), _pad_rank(a2, 0), row(a0), seq, tm=256)
    mixed = _wkv(rkvg.reshape(4, batch, seq, D_MODEL), lw.reshape(batch, seq, D_MODEL),
                 a.reshape(batch, seq, D_MODEL), row(k_k), row(k_a), row(r_k),
                 row(gn_w), row(gn_b), chunks=8)
    return _odd_tail(mixed.reshape(m, D_MODEL), x2, w_o.astype(BF16), row(ln_g), row(ln_b), tm=256)


def kernel(x, ev_w_in, ev_w_pool, ev_pool_scale, ev_w_out, od_mu, od_w_r, od_w_k, od_w_v, od_w_g,
           od_w0, od_w1, od_w2, od_a0, od_a1, od_a2, od_k_k, od_k_a, od_r_k, od_gn_w, od_gn_b,
           od_w_o, ln_g, ln_b):
    batch, seq, d = x.shape
    x2 = x.reshape(batch * seq, d)
    for layer in range(N_LAYERS):
        j = layer // 2
        if layer % 2 == 0:
            x2 = _even_layer(x2, seq, ev_w_in[j], ev_w_pool[j], ev_pool_scale[j], ev_w_out[j],
                             ln_g[layer], ln_b[layer])
        else:
            x2 = _odd_layer(x2, seq, od_mu[j], od_w_r[j], od_w_k[j], od_w_v[j], od_w_g[j],
                            od_w0[j], od_w1[j], od_w2[j], od_a0[j], od_a1[j], od_a2[j],
                            od_k_k[j], od_k_a[j], od_r_k[j], od_gn_w[j], od_gn_b[j], od_w_o[j],
                            ln_g[layer], ln_b[layer])
    return x2.reshape(batch, seq, d)
```

```python
import functools
import math

import jax
import jax.numpy as jnp
from jax import lax
from jax.experimental import pallas as pl
from jax.experimental.pallas import tpu as pltpu

F32 = jnp.float32
BF16 = jnp.bfloat16

D_MODEL = 2048
N_LAYERS = 2
SB_HEADS = 8
SB_HEAD_DIM = 128
SB_WIDTH = SB_HEADS * SB_HEAD_DIM
POOL_WINDOWS = (2, 4, 8, 16)
POOL_GROUP_DIM = 256
POOL_WIDTH = len(POOL_WINDOWS) * POOL_GROUP_DIM
POOL_HALO = 16
RWKV_HEAD_DIM = 64
LORA_PAD = 128
GN_EPS = 64e-5
L2_EPS = 1e-12
LN_EPS = 1e-5
DEEPNORM_ALPHA = (2 * N_LAYERS) ** 0.25

SUBLANES = 8
LANES = 128
VMEM_LIMIT = 52 * 1024 * 1024

EXP_UNDERFLOW = -104.0
DECAY_SCALE = math.exp(-0.5)

WKV_CHUNK = 64
WKV_HEADS = 4
WKV_LANES = WKV_HEADS * RWKV_HEAD_DIM
assert WKV_CHUNK == RWKV_HEAD_DIM


def _dot(a, b):
    return jnp.dot(a.astype(BF16), b.astype(BF16), preferred_element_type=F32)


def _dot_nt(a, b):
    return lax.dot_general(a.astype(BF16), b.astype(BF16), (((1,), (1,)), ((), ())),
                           preferred_element_type=F32)


def _dot_tn(a, b):
    return lax.dot_general(a.astype(BF16), b.astype(BF16), (((0,), (0,)), ((), ())),
                           preferred_element_type=F32)


def _silu(g):
    return g / (1.0 + jnp.exp(-g))


def _layer_norm_rows(z, g, b):
    mean = jnp.mean(z, axis=-1, keepdims=True)
    d = z - mean
    var = jnp.mean(d * d, axis=-1, keepdims=True)
    return d * lax.rsqrt(var + LN_EPS) * g + b


def _params(*sem):
    return pltpu.CompilerParams(dimension_semantics=sem, vmem_limit_bytes=VMEM_LIMIT)


def _proj_kernel(x_ref, w_ref, s_ref, o_ref, xb_ref):
    @pl.when(pl.program_id(1) == 0)
    def _():
        xb_ref[...] = x_ref[...].astype(BF16)

    acc = jnp.dot(xb_ref[...], w_ref[...], preferred_element_type=F32)
    o_ref[...] = (acc * s_ref[...]).astype(o_ref.dtype)


def _project(x, w, col_scale, out_dtype, tm, tn):
    m, k = x.shape
    n = w.shape[1]
    return pl.pallas_call(
        _proj_kernel,
        grid=(m // tm, n // tn),
        in_specs=[pl.BlockSpec((tm, k), lambda i, j: (i, 0)),
                  pl.BlockSpec((k, tn), lambda i, j: (0, j)),
                  pl.BlockSpec((1, tn), lambda i, j: (0, j))],
        out_specs=pl.BlockSpec((tm, tn), lambda i, j: (i, j)),
        out_shape=jax.ShapeDtypeStruct((m, n), out_dtype),
        scratch_shapes=[pltpu.VMEM((tm, k), BF16)],
        compiler_params=_params("parallel", "arbitrary"),
        name="in_proj",
    )(x, w, col_scale)


def _attn_kernel(q_ref, k_ref, v_ref, g_ref, o_ref, *, tq, tk):
    i = pl.program_id(2)
    inner = tq // tk
    krow = lax.broadcasted_iota(jnp.int32, (tk, tk), 0)
    kcol = lax.broadcasted_iota(jnp.int32, (tk, tk), 1)
    after = jnp.where(krow > kcol, 1.0, 0.0).astype(BF16)

    def block(q, start, carry, acc, diagonal):
        kb = k_ref[0, pl.ds(start, tk), :]
        vb = v_ref[0, pl.ds(start, tk), :]
        z = _dot_nt(q, kb)
        log_beta = jnp.minimum(z, 0.0) - jnp.log(1.0 + jnp.exp(-jnp.abs(z)))
        log_keep = log_beta - z
        if diagonal:
            causal = (lax.broadcasted_iota(jnp.int32, z.shape, 1)
                      < lax.broadcasted_iota(jnp.int32, z.shape, 0))
            log_keep = jnp.where(causal, log_keep, 0.0)
        later = jnp.dot(log_keep.astype(BF16), after, preferred_element_type=F32) + carry
        w = jnp.exp(log_beta + later)
        if diagonal:
            w = jnp.where(causal, w, 0.0)
        acc = acc + _dot(w, vb)
        carry = carry + jnp.sum(log_keep, axis=-1, keepdims=True)
        return carry, acc

    carry = jnp.zeros((tq, 1), F32)
    acc = jnp.zeros((tq, SB_HEAD_DIM), F32)
    for jo in reversed(range(inner)):
        lo_row = jo * tk
        start = pl.multiple_of(i * tq + lo_row, tk)
        c_new, a_new = block(q_ref[0, lo_row:tq, :], start, carry[lo_row:tq], acc[lo_row:tq], True)
        carry = jnp.concatenate([carry[0:lo_row], c_new], axis=0) if lo_row else c_new
        acc = jnp.concatenate([acc[0:lo_row], a_new], axis=0) if lo_row else a_new

    def cond(state):
        j, carry, _ = state
        return jnp.logical_and(j >= 0, jnp.max(carry) > EXP_UNDERFLOW)

    def body(state):
        j, carry, acc = state
        carry, acc = block(q_ref[0], pl.multiple_of(j * tk, tk), carry, acc, False)
        return j - 1, carry, acc

    _, _, acc = lax.while_loop(cond, body, (i * inner - 1, carry, acc))
    o_ref[0] = (acc * _silu(g_ref[0])).astype(o_ref.dtype)


def _attention(qkv, gu, tq, tk):
    b, s, _ = qkv.shape
    kernel = functools.partial(_attn_kernel, tq=tq, tk=tk)
    return pl.pallas_call(
        kernel,
        grid=(b, SB_HEADS, s // tq),
        in_specs=[pl.BlockSpec((1, tq, SB_HEAD_DIM), lambda bi, h, i: (bi, i, h)),
                  pl.BlockSpec((1, s, SB_HEAD_DIM), lambda bi, h, i: (bi, 0, SB_HEADS + h)),
                  pl.BlockSpec((1, s, SB_HEAD_DIM), lambda bi, h, i: (bi, 0, 2 * SB_HEADS + h)),
                  pl.BlockSpec((1, tq, SB_HEAD_DIM), lambda bi, h, i: (bi, i, h))],
        out_specs=pl.BlockSpec((1, tq, SB_HEAD_DIM), lambda bi, h, i: (bi, i, h)),
        out_shape=jax.ShapeDtypeStruct((b, s, SB_WIDTH), BF16),
        compiler_params=_params("parallel", "parallel", "arbitrary"),
        name="stickbreak_attn",
    )(qkv, qkv, qkv, gu)


def _even_tail_kernel(ma_ref, u_ref, halo_ref, gb_ref, x_ref, wp_ref, ps_ref, wo_ref,
                      lg_ref, lb_ref, o_ref, *, tm, seq):
    i = pl.program_id(0)
    first = (i * tm) % seq == 0
    u = u_ref[...]
    halo = jnp.where(first, 0.0, halo_ref[...])
    ext = jnp.concatenate([halo, u], axis=0)
    pos = (i * tm) % seq + lax.broadcasted_iota(jnp.int32, (tm, 1), 0)

    y = jnp.dot(ma_ref[...], wo_ref[0:SB_WIDTH, :], preferred_element_type=F32)
    for g, window in enumerate(POOL_WINDOWS):
        lo = g * POOL_GROUP_DIM
        hi = lo + POOL_GROUP_DIM
        acc = ext[:, lo:hi]
        span = 1
        while span < window:
            acc = acc + pltpu.roll(acc, span, 0)
            span *= 2
        count = jnp.minimum(pos + 1, window).astype(F32)
        pooled = acc[POOL_HALO:, :] / count - u[:, lo:hi]
        mixed = _dot(pooled, wp_ref[g]) * ps_ref[:, lo:hi]
        mb = mixed * _silu(gb_ref[:, lo:hi])
        y = y + _dot(mb, wo_ref[SB_WIDTH + lo:SB_WIDTH + hi, :])

    z = DEEPNORM_ALPHA * x_ref[...] + y
    o_ref[...] = _layer_norm_rows(z, lg_ref[...], lb_ref[...])


def _even_tail(ma, gu, x, w_pool, pool_scale, w_out, ln_g, ln_b, seq, tm):
    m = x.shape[0]
    halo_blocks = tm // POOL_HALO
    u_col = SB_WIDTH // POOL_WIDTH
    kernel = functools.partial(_even_tail_kernel, tm=tm, seq=seq)
    return pl.pallas_call(
        kernel,
        grid=(m // tm,),
        in_specs=[pl.BlockSpec((tm, SB_WIDTH), lambda i: (i, 0)),
                  pl.BlockSpec((tm, POOL_WIDTH), lambda i: (i, u_col)),
                  pl.BlockSpec((POOL_HALO, POOL_WIDTH),
                               lambda i: (jnp.maximum(i * halo_blocks - 1, 0), u_col)),
                  pl.BlockSpec((tm, POOL_WIDTH), lambda i: (i, u_col + 1)),
                  pl.BlockSpec((tm, D_MODEL), lambda i: (i, 0)),
                  pl.BlockSpec(w_pool.shape, lambda i: (0, 0, 0)),
                  pl.BlockSpec((1, POOL_WIDTH), lambda i: (0, 0)),
                  pl.BlockSpec(w_out.shape, lambda i: (0, 0)),
                  pl.BlockSpec((1, D_MODEL), lambda i: (0, 0)),
                  pl.BlockSpec((1, D_MODEL), lambda i: (0, 0))],
        out_specs=pl.BlockSpec((tm, D_MODEL), lambda i: (i, 0)),
        out_shape=jax.ShapeDtypeStruct((m, D_MODEL), F32),
        compiler_params=_params("parallel"),
        name="even_tail",
    )(ma, gu, gu, gu, x, w_pool, pool_scale, w_out, ln_g, ln_b)


def _shifted_delta(x, halo_ref, first):
    last = jnp.where(first, 0.0, halo_ref[SUBLANES - 1:SUBLANES, :])
    row = lax.broadcasted_iota(jnp.int32, (x.shape[0], 1), 0)
    prev = jnp.where(row == 0, last, pltpu.roll(x, 1, 0))
    return prev - x


def _shift_proj_kernel(x_ref, halo_ref, mu_ref, w_ref, o_ref, *, tm, seq):
    i = pl.program_id(1)
    x = x_ref[...]
    xx = _shifted_delta(x, halo_ref, (i * tm) % seq == 0)
    o_ref[0] = _dot(x + xx * mu_ref[0], w_ref[0])


def _shift_project(x, mu4, w4, seq, tm):
    m = x.shape[0]
    n_proj = w4.shape[0]
    halo_blocks = tm // SUBLANES
    kernel = functools.partial(_shift_proj_kernel, tm=tm, seq=seq)
    return pl.pallas_call(
        kernel,
        grid=(n_proj, m // tm),
        in_specs=[pl.BlockSpec((tm, D_MODEL), lambda j, i: (i, 0)),
                  pl.BlockSpec((SUBLANES, D_MODEL),
                               lambda j, i: (jnp.maximum(i * halo_blocks - 1, 0), 0)),
                  pl.BlockSpec((1, 1, D_MODEL), lambda j, i: (j, 0, 0)),
                  pl.BlockSpec((1, D_MODEL, D_MODEL), lambda j, i: (j, 0, 0))],
        out_specs=pl.BlockSpec((1, tm, D_MODEL), lambda j, i: (j, i, 0)),
        out_shape=jax.ShapeDtypeStruct((n_proj, m, D_MODEL), F32),
        compiler_params=_params("parallel", "parallel"),
        name="shift_proj",
    )(x, x, mu4, w4)


def _lora_kernel(x_ref, halo_ref, mu_ref, w1_ref, w2_ref, w0_ref, a1_ref, a2_ref, a0_ref,
                 lw_ref, a_ref, *, tm, seq):
    i = pl.program_id(0)
    x = x_ref[...]
    xx = _shifted_delta(x, halo_ref, (i * tm) % seq == 0)
    xw = x + xx * mu_ref[0:1, :]
    xa = x + xx * mu_ref[1:2, :]
    ww = w0_ref[...] + _dot(jnp.tanh(_dot(xw, w1_ref[...])), w2_ref[...])
    lw_ref[...] = -DECAY_SCALE / (1.0 + jnp.exp(-ww))
    aa = a0_ref[...] + _dot(_dot(xa, a1_ref[...]), a2_ref[...])
    a_ref[...] = 1.0 / (1.0 + jnp.exp(-aa))


def _lora(x, mu2, w1, w2, w0, a1, a2, a0, seq, tm):
    m = x.shape[0]
    halo_blocks = tm // SUBLANES
    kernel = functools.partial(_lora_kernel, tm=tm, seq=seq)
    full = lambda arr: pl.BlockSpec(arr.shape, lambda i: (0,) * arr.ndim)
    out = jax.ShapeDtypeStruct((m, D_MODEL), F32)
    return pl.pallas_call(
        kernel,
        grid=(m // tm,),
        in_specs=[pl.BlockSpec((tm, D_MODEL), lambda i: (i, 0)),
                  pl.BlockSpec((SUBLANES, D_MODEL),
                               lambda i: (jnp.maximum(i * halo_blocks - 1, 0), 0)),
                  full(mu2), full(w1), full(w2), full(w0), full(a1), full(a2), full(a0)],
        out_specs=[pl.BlockSpec((tm, D_MODEL), lambda i: (i, 0)),
                   pl.BlockSpec((tm, D_MODEL), lambda i: (i, 0))],
        out_shape=[out, out],
        compiler_params=_params("parallel"),
        name="decay_iclr_lora",
    )(x, x, mu2, w1, w2, w0, a1, a2, a0)


def _blockdiag(y, bd_mask):
    tiled = jnp.concatenate([y.astype(BF16)] * WKV_HEADS, axis=0)
    return tiled * bd_mask


def _fold_diag_blocks(full, head_masks):
    c = WKV_CHUNK
    out = jnp.where(head_masks[0], full[0:c, :], 0.0)
    for h in range(1, WKV_HEADS):
        out = out + jnp.where(head_masks[h], full[h * c:(h + 1) * c, :], 0.0)
    return out


def _wkv_kernel(r_ref, k_ref, v_ref, g_ref, lw_ref, a_ref, kk_ref, ka_ref, rk_ref,
                gw_ref, gb_ref, o_ref, state_ref, *, chunks, quads):
    @pl.when(pl.program_id(2) == 0)
    def _():
        state_ref[...] = jnp.zeros_like(state_ref)

    c = WKV_CHUNK
    w = WKV_LANES
    t = chunks * c
    srow = lax.broadcasted_iota(jnp.int32, (w, w), 0)
    scol = lax.broadcasted_iota(jnp.int32, (w, w), 1)
    bd_mask = jnp.where((srow // c) == (scol // c), 1.0, 0.0).astype(BF16)
    lane = lax.broadcasted_iota(jnp.int32, (1, w), 1)
    head_masks = [(lane // c) == h for h in range(WKV_HEADS)]
    crow = lax.broadcasted_iota(jnp.int32, (c, w), 0)
    ccol = lax.broadcasted_iota(jnp.int32, (c, w), 1) % c
    strict = crow > ccol
    lower = crow >= ccol
    eye_cat = jnp.where(crow == ccol, 1.0, 0.0)
    tg = min(t, w)
    trow = lax.broadcasted_iota(jnp.int32, (tg, tg), 0)
    tcol = lax.broadcasted_iota(jnp.int32, (tg, tg), 1)
    prefix = jnp.where(jnp.logical_and((trow // c) == (tcol // c), trow >= tcol),
                       1.0, 0.0).astype(BF16)

    def prepare(qi):
        lanes = slice(qi * w, (qi + 1) * w)
        r = r_ref[0, 0, :, lanes]
        k = k_ref[0, 0, :, lanes]
        v = v_ref[0, 0, :, lanes]
        lw = lw_ref[0, :, lanes]
        a = a_ref[0, :, lanes]

        kk = k * kk_ref[:, lanes]
        kmod = k * (1.0 + (a - 1.0) * ka_ref[:, lanes])
        sums = _dot(jnp.concatenate([kk * kk, r * kmod * rk_ref[:, lanes]], axis=0), bd_mask)
        kk = kk / jnp.maximum(jnp.sqrt(sums[0:t]), L2_EPS)
        b = a * kk

        lw_hi = lw.astype(BF16)
        lw_lo = (lw - lw_hi.astype(F32)).astype(BF16)
        lw_split = jnp.concatenate([lw_hi, lw_lo], axis=1)
        cum2 = jnp.concatenate([jnp.dot(prefix, lw_split[s:s + tg], preferred_element_type=F32)
                                for s in range(0, t, tg)], axis=0)
        cum = cum2[:, 0:w] + cum2[:, w:2 * w]
        totals = [cum[(ci + 1) * c - 1:(ci + 1) * c, :] for ci in range(chunks)]
        total_rows = jnp.concatenate([jnp.broadcast_to(tot, (c, w)) for tot in totals], axis=0)
        g_inv = jnp.exp(-cum)
        g_tail = jnp.exp(total_rows - cum)
        whole = dict(kkg=kk * jnp.exp(cum - lw), rg=r * jnp.exp(cum), kd=kmod * g_inv,
                     bd=b * g_inv, kt=kmod * g_tail, bt=b * g_tail, v=v)
        per_chunk = [{name: x[ci * c:(ci + 1) * c] for name, x in whole.items()}
                     for ci in range(chunks)]
        for ci in range(chunks):
            per_chunk[ci]["total"] = totals[ci]
        return per_chunk, sums[t:2 * t] * v

    prepared = [prepare(qi) for qi in range(quads)]
    units = [(qi, ci) for ci in range(chunks) for qi in range(quads)]
    every = range(len(units))
    un = [prepared[qi][0][ci] for qi, ci in units]
    bdm = lambda y: _blockdiag(y, bd_mask)

    lhs = [jnp.concatenate([un[ci]["kkg"], un[ci]["rg"]], axis=0) for ci in every]
    a1 = [_dot_nt(lhs[ci], bdm(un[ci]["bd"])) for ci in every]
    a2 = [_dot_nt(lhs[ci], bdm(un[ci]["kd"])) for ci in every]
    a_kb = [jnp.where(strict, a1[ci][0:c], 0.0) for ci in every]
    a_rb = [jnp.where(lower, a1[ci][c:2 * c], 0.0) for ci in every]
    a_kk = [jnp.where(strict, a2[ci][0:c], 0.0) for ci in every]
    a_rk = [jnp.where(lower, a2[ci][c:2 * c], 0.0) for ci in every]

    power = [-a_kb[ci] for ci in every]
    inv = [eye_cat + power[ci] for ci in every]
    power = [_dot(power[ci], bdm(power[ci])) for ci in every]
    span = 2
    while 2 * span < c:
        both = [_dot(jnp.concatenate([inv[ci], power[ci]], axis=0), bdm(power[ci])) for ci in every]
        inv = [inv[ci] + both[ci][0:c] for ci in every]
        power = [both[ci][c:2 * c] for ci in every]
        span *= 2
    inv = [inv[ci] + _dot(inv[ci], bdm(power[ci])) for ci in every]

    av = [_dot(jnp.concatenate([a_kk[ci], a_rk[ci]], axis=0), bdm(un[ci]["v"])) for ci in every]
    akkv = [av[ci][0:c] for ci in every]
    arkv = [av[ci][c:2 * c] for ci in every]
    wu = [_dot(inv[ci], jnp.concatenate([bdm(un[ci]["kkg"]), bdm(akkv[ci])], axis=1)) for ci in every]
    w1 = [wu[ci][:, 0:w] for ci in every]
    u0 = [wu[ci][:, w:2 * w] for ci in every]
    arb = [_dot(a_rb[ci], jnp.concatenate([bdm(w1[ci]), bdm(u0[ci])], axis=1)) for ci in every]
    r2 = [un[ci]["rg"] - arb[ci][:, 0:w] for ci in every]
    o0 = [arkv[ci] - arb[ci][:, w:2 * w] for ci in every]
    trans = [jnp.where(crow == ccol, jnp.exp(un[ci]["total"]), 0.0)
             - _fold_diag_blocks(_dot_tn(un[ci]["bt"], w1[ci]), head_masks) for ci in every]
    fresh = [_fold_diag_blocks(
        _dot_tn(jnp.concatenate([un[ci]["kt"], -un[ci]["bt"]], axis=0),
                jnp.concatenate([un[ci]["v"], u0[ci]], axis=0)), head_masks) for ci in every]

    states = [state_ref[qi] for qi in range(quads)]
    outs = [[] for _ in range(quads)]
    for ui, (qi, _) in enumerate(units):
        both = _dot(jnp.concatenate([trans[ui], r2[ui]], axis=0), bdm(states[qi]))
        states[qi] = both[0:c] + fresh[ui]
        outs[qi].append(both[c:2 * c] + o0[ui])

    inv_n = 1.0 / RWKV_HEAD_DIM
    for qi in range(quads):
        lanes = slice(qi * w, (qi + 1) * w)
        state_ref[qi] = states[qi]
        o = jnp.concatenate(outs[qi], axis=0)
        d = o - _dot(o, bd_mask) * inv_n
        var = _dot(d * d, bd_mask) * inv_n
        normed = d * lax.rsqrt(var + GN_EPS) * gw_ref[:, lanes] + gb_ref[:, lanes]
        gate = _silu(g_ref[0, 0, :, lanes])
        o_ref[0, :, lanes] = ((normed + prepared[qi][1]) * gate).astype(o_ref.dtype)


def _wkv(rkvg, lw, a, k_k, k_a, r_k, gn_w, gn_b, chunks, quads):
    _, b, s, d = rkvg.shape
    t = chunks * WKV_CHUNK
    width = quads * WKV_LANES
    kernel = functools.partial(_wkv_kernel, chunks=chunks, quads=quads)
    proj = lambda j: pl.BlockSpec((1, 1, t, width), lambda bi, h, ci: (j, bi, ci, h))
    tok = pl.BlockSpec((1, t, width), lambda bi, h, ci: (bi, ci, h))
    vec = pl.BlockSpec((1, width), lambda bi, h, ci: (0, h))
    return pl.pallas_call(
        kernel,
        grid=(b, d // width, s // t),
        in_specs=[proj(0), proj(1), proj(2), proj(3), tok, tok, vec, vec, vec, vec, vec],
        out_specs=tok,
        out_shape=jax.ShapeDtypeStruct((b, s, d), BF16),
        scratch_shapes=[pltpu.VMEM((quads, RWKV_HEAD_DIM, WKV_LANES), F32)],
        compiler_params=_params("parallel", "parallel", "arbitrary"),
        name="wkv7_chunked",
    )(rkvg, rkvg, rkvg, rkvg, lw, a, k_k, k_a, r_k, gn_w, gn_b)


def _odd_tail_kernel(m_ref, x_ref, wo_ref, lg_ref, lb_ref, o_ref):
    y = jnp.dot(m_ref[...], wo_ref[...], preferred_element_type=F32)
    z = DEEPNORM_ALPHA * x_ref[...] + y
    o_ref[...] = _layer_norm_rows(z, lg_ref[...], lb_ref[...])


def _odd_tail(mixed, x, w_o, ln_g, ln_b, tm):
    m = x.shape[0]
    return pl.pallas_call(
        _odd_tail_kernel,
        grid=(m // tm,),
        in_specs=[pl.BlockSpec((tm, D_MODEL), lambda i: (i, 0)),
                  pl.BlockSpec((tm, D_MODEL), lambda i: (i, 0)),
                  pl.BlockSpec(w_o.shape, lambda i: (0, 0)),
                  pl.BlockSpec((1, D_MODEL), lambda i: (0, 0)),
                  pl.BlockSpec((1, D_MODEL), lambda i: (0, 0))],
        out_specs=pl.BlockSpec((tm, D_MODEL), lambda i: (i, 0)),
        out_shape=jax.ShapeDtypeStruct((m, D_MODEL), F32),
        compiler_params=_params("parallel"),
        name="odd_tail",
    )(mixed, x, w_o, ln_g, ln_b)


def _even_layer(x2, seq, w_in, w_pool, pool_scale, w_out, ln_g, ln_b):
    m = x2.shape[0]
    batch = m // seq
    split = 3 * SB_WIDTH
    w_in = w_in.astype(BF16)
    q_scale = jnp.concatenate([jnp.full((1, SB_WIDTH), SB_HEAD_DIM ** -0.5, F32),
                               jnp.ones((1, split - SB_WIDTH), F32)], axis=1)
    qkv = _project(x2, w_in[:, :split], q_scale, BF16, tm=1024, tn=1024)
    gu = _project(x2, w_in[:, split:], jnp.ones((1, split), F32), F32, tm=1024, tn=1024)
    ma = _attention(qkv.reshape(batch, seq, split), gu.reshape(batch, seq, split), tq=512, tk=256)
    return _even_tail(ma.reshape(m, SB_WIDTH), gu, x2, w_pool.astype(BF16),
                      pool_scale.reshape(1, POOL_WIDTH), w_out.astype(BF16),
                      ln_g.reshape(1, D_MODEL), ln_b.reshape(1, D_MODEL), seq, tm=512)


def _pad_rank(w, axis):
    pad = [(0, 0), (0, 0)]
    pad[axis] = (0, LORA_PAD - w.shape[axis])
    return jnp.pad(w, pad).astype(BF16)


def _odd_layer(x2, seq, mu, w_r, w_k, w_v, w_g, w0, w1, w2, a0, a1, a2, k_k, k_a, r_k,
               gn_w, gn_b, w_o, ln_g, ln_b):
    m = x2.shape[0]
    batch = m // seq
    row = lambda vec: vec.reshape(1, D_MODEL)
    mu4 = jnp.stack([mu[0], mu[2], mu[3], mu[5]]).reshape(4, 1, D_MODEL)
    w4 = jnp.stack([w_r, w_k, w_v, w_g]).astype(BF16)
    rkvg = _shift_project(x2, mu4, w4, seq, tm=512)
    mu2 = jnp.stack([mu[1], mu[4]])
    lw, a = _lora(x2, mu2, _pad_rank(w1, 1), _pad_rank(w2, 0), row(w0),
                  _pad_rank(a1, 1), _pad_rank(a2, 0), row(a0), seq, tm=256)
    mixed = _wkv(rkvg.reshape(4, batch, seq, D_MODEL), lw.reshape(batch, seq, D_MODEL),
                 a.reshape(batch, seq, D_MODEL), row(k_k), row(k_a), row(r_k),
                 row(gn_w), row(gn_b), chunks=4, quads=4)
    return _odd_tail(mixed.reshape(m, D_MODEL), x2, w_o.astype(BF16), row(ln_g), row(ln_b), tm=512)


def kernel(x, ev_w_in, ev_w_pool, ev_pool_scale, ev_w_out, od_mu, od_w_r, od_w_k, od_w_v, od_w_g,
           od_w0, od_w1, od_w2, od_a0, od_a1, od_a2, od_k_k, od_k_a, od_r_k, od_gn_w, od_gn_b,
           od_w_o, ln_g, ln_b):
    batch, seq, d = x.shape
    x2 = x.reshape(batch * seq, d)
    for layer in range(N_LAYERS):
        j = layer // 2
        if layer % 2 == 0:
            x2 = _even_layer(x2, seq, ev_w_in[j], ev_w_pool[j], ev_pool_scale[j], ev_w_out[j],
                             ln_g[layer], ln_b[layer])
        else:
            x2 = _odd_layer(x2, seq, od_mu[j], od_w_r[j], od_w_k[j], od_w_v[j], od_w_g[j],
                            od_w0[j], od_w1[j], od_w2[j], od_a0[j], od_a1[j], od_a2[j],
                            od_k_k[j], od_k_a[j], od_r_k[j], od_gn_w[j], od_gn_b[j], od_w_o[j],
                            ln_g[layer], ln_b[layer])
    return x2.reshape(batch, seq, d)
```

```python
import functools
import math

import jax
import jax.numpy as jnp
from jax import lax
from jax.experimental import pallas as pl
from jax.experimental.pallas import tpu as pltpu

F32 = jnp.float32
BF16 = jnp.bfloat16

D_MODEL = 2048
N_LAYERS = 2
SB_HEADS = 8
SB_HEAD_DIM = 128
SB_WIDTH = SB_HEADS * SB_HEAD_DIM
POOL_WINDOWS = (2, 4, 8, 16)
POOL_GROUP_DIM = 256
POOL_WIDTH = len(POOL_WINDOWS) * POOL_GROUP_DIM
POOL_HALO = 16
RWKV_HEAD_DIM = 64
LORA_PAD = 128
GN_EPS = 64e-5
L2_EPS = 1e-12
LN_EPS = 1e-5
DEEPNORM_ALPHA = (2 * N_LAYERS) ** 0.25

SUBLANES = 8
LANES = 128
VMEM_LIMIT = 52 * 1024 * 1024

EXP_UNDERFLOW = -104.0
DECAY_SCALE = math.exp(-0.5)

WKV_CHUNK = 64
WKV_HEADS = 4
WKV_LANES = WKV_HEADS * RWKV_HEAD_DIM
assert WKV_CHUNK == RWKV_HEAD_DIM


def _dot(a, b):
    return jnp.dot(a.astype(BF16), b.astype(BF16), preferred_element_type=F32)


def _dot_nt(a, b):
    return lax.dot_general(a.astype(BF16), b.astype(BF16), (((1,), (1,)), ((), ())),
                           preferred_element_type=F32)


def _dot_tn(a, b):
    return lax.dot_general(a.astype(BF16), b.astype(BF16), (((0,), (0,)), ((), ())),
                           preferred_element_type=F32)


def _silu(g):
    return g / (1.0 + jnp.exp(-g))


def _layer_norm_rows(z, g, b):
    mean = jnp.mean(z, axis=-1, keepdims=True)
    d = z - mean
    var = jnp.mean(d * d, axis=-1, keepdims=True)
    return d * lax.rsqrt(var + LN_EPS) * g + b


def _params(*sem):
    return pltpu.CompilerParams(dimension_semantics=sem, vmem_limit_bytes=VMEM_LIMIT)


def _proj_kernel(x_ref, w_ref, s_ref, o_ref, xb_ref):
    @pl.when(pl.program_id(1) == 0)
    def _():
        xb_ref[...] = x_ref[...].astype(BF16)

    acc = jnp.dot(xb_ref[...], w_ref[...], preferred_element_type=F32)
    o_ref[...] = (acc * s_ref[...]).astype(o_ref.dtype)


def _project(x, w, col_scale, out_dtype, tm, tn):
    m, k = x.shape
    n = w.shape[1]
    return pl.pallas_call(
        _proj_kernel,
        grid=(m // tm, n // tn),
        in_specs=[pl.BlockSpec((tm, k), lambda i, j: (i, 0)),
                  pl.BlockSpec((k, tn), lambda i, j: (0, j)),
                  pl.BlockSpec((1, tn), lambda i, j: (0, j))],
        out_specs=pl.BlockSpec((tm, tn), lambda i, j: (i, j)),
        out_shape=jax.ShapeDtypeStruct((m, n), out_dtype),
        scratch_shapes=[pltpu.VMEM((tm, k), BF16)],
        compiler_params=_params("parallel", "arbitrary"),
        name="in_proj",
    )(x, w, col_scale)


def _attn_kernel(q_ref, k_ref, v_ref, g_ref, o_ref, *, tq, tk, heads):
    i = pl.program_id(2)
    inner = tq // tk
    krow = lax.broadcasted_iota(jnp.int32, (tk, tk), 0)
    kcol = lax.broadcasted_iota(jnp.int32, (tk, tk), 1)
    after = jnp.where(krow > kcol, 1.0, 0.0).astype(BF16)
    lanes = [slice(h * SB_HEAD_DIM, (h + 1) * SB_HEAD_DIM) for h in range(heads)]

    def block(h, rows, start, carry, acc, diagonal):
        q = q_ref[0, rows, lanes[h]]
        kb = k_ref[0, pl.ds(start, tk), lanes[h]]
        vb = v_ref[0, pl.ds(start, tk), lanes[h]]
        z = _dot_nt(q, kb)
        log_beta = jnp.minimum(z, 0.0) - jnp.log(1.0 + jnp.exp(-jnp.abs(z)))
        log_keep = log_beta - z
        if diagonal:
            causal = (lax.broadcasted_iota(jnp.int32, z.shape, 1)
                      < lax.broadcasted_iota(jnp.int32, z.shape, 0))
            log_keep = jnp.where(causal, log_keep, 0.0)
        later = jnp.dot(log_keep.astype(BF16), after, preferred_element_type=F32) + carry
        w = jnp.exp(log_beta + later)
        if diagonal:
            w = jnp.where(causal, w, 0.0)
        acc = acc + _dot(w, vb)
        carry = carry + jnp.sum(log_keep, axis=-1, keepdims=True)
        return carry, acc

    carry = [jnp.zeros((tq, 1), F32) for _ in range(heads)]
    acc = [jnp.zeros((tq, SB_HEAD_DIM), F32) for _ in range(heads)]
    for jo in reversed(range(inner)):
        lo_row = jo * tk
        start = pl.multiple_of(i * tq + lo_row, tk)
        for h in range(heads):
            c_new, a_new = block(h, slice(lo_row, tq), start, carry[h][lo_row:tq],
                                 acc[h][lo_row:tq], True)
            carry[h] = jnp.concatenate([carry[h][0:lo_row], c_new], axis=0) if lo_row else c_new
            acc[h] = jnp.concatenate([acc[h][0:lo_row], a_new], axis=0) if lo_row else a_new

    def cond(state):
        j, carries, _ = state
        worst = functools.reduce(jnp.maximum, carries)
        return jnp.logical_and(j >= 0, jnp.max(worst) > EXP_UNDERFLOW)

    for lo_row in range(0, tq, tk):
        rows = slice(lo_row, lo_row + tk)

        def body(state, rows=rows):
            j, carries, accs = state
            start = pl.multiple_of(j * tk, tk)
            done = [block(h, rows, start, carries[h], accs[h], False) for h in range(heads)]
            return j - 1, tuple(c for c, _ in done), tuple(a for _, a in done)

        first = (i * inner - 1, tuple(c[rows] for c in carry), tuple(a[rows] for a in acc))
        _, _, parts = lax.while_loop(cond, body, first)
        for h in range(heads):
            gate = _silu(g_ref[0, rows, lanes[h]])
            o_ref[0, rows, lanes[h]] = (parts[h] * gate).astype(o_ref.dtype)


def _attention(qkv, gu, tq, tk, heads):
    b, s, _ = qkv.shape
    width = heads * SB_HEAD_DIM
    groups = SB_HEADS // heads
    kernel = functools.partial(_attn_kernel, tq=tq, tk=tk, heads=heads)
    return pl.pallas_call(
        kernel,
        grid=(b, groups, s // tq),
        in_specs=[pl.BlockSpec((1, tq, width), lambda bi, h, i: (bi, i, h)),
                  pl.BlockSpec((1, s, width), lambda bi, h, i: (bi, 0, groups + h)),
                  pl.BlockSpec((1, s, width), lambda bi, h, i: (bi, 0, 2 * groups + h)),
                  pl.BlockSpec((1, tq, width), lambda bi, h, i: (bi, i, h))],
        out_specs=pl.BlockSpec((1, tq, width), lambda bi, h, i: (bi, i, h)),
        out_shape=jax.ShapeDtypeStruct((b, s, SB_WIDTH), BF16),
        compiler_params=_params("parallel", "parallel", "arbitrary"),
        name="stickbreak_attn",
    )(qkv, qkv, qkv, gu)


def _even_tail_kernel(ma_ref, u_ref, halo_ref, gb_ref, x_ref, wp_ref, ps_ref, wo_ref,
                      lg_ref, lb_ref, o_ref, *, tm, seq):
    i = pl.program_id(0)
    first = (i * tm) % seq == 0
    u = u_ref[...]
    halo = jnp.where(first, 0.0, halo_ref[...])
    ext = jnp.concatenate([halo, u], axis=0)
    pos = (i * tm) % seq + lax.broadcasted_iota(jnp.int32, (tm, 1), 0)

    y = jnp.dot(ma_ref[...], wo_ref[0:SB_WIDTH, :], preferred_element_type=F32)
    for g, window in enumerate(POOL_WINDOWS):
        lo = g * POOL_GROUP_DIM
        hi = lo + POOL_GROUP_DIM
        acc = ext[:, lo:hi]
        span = 1
        while span < window:
            acc = acc + pltpu.roll(acc, span, 0)
            span *= 2
        count = jnp.minimum(pos + 1, window).astype(F32)
        pooled = acc[POOL_HALO:, :] / count - u[:, lo:hi]
        mixed = _dot(pooled, wp_ref[g]) * ps_ref[:, lo:hi]
        mb = mixed * _silu(gb_ref[:, lo:hi])
        y = y + _dot(mb, wo_ref[SB_WIDTH + lo:SB_WIDTH + hi, :])

    z = DEEPNORM_ALPHA * x_ref[...] + y
    o_ref[...] = _layer_norm_rows(z, lg_ref[...], lb_ref[...])


def _even_tail(ma, gu, x, w_pool, pool_scale, w_out, ln_g, ln_b, seq, tm):
    m = x.shape[0]
    halo_blocks = tm // POOL_HALO
    u_col = SB_WIDTH // POOL_WIDTH
    kernel = functools.partial(_even_tail_kernel, tm=tm, seq=seq)
    return pl.pallas_call(
        kernel,
        grid=(m // tm,),
        in_specs=[pl.BlockSpec((tm, SB_WIDTH), lambda i: (i, 0)),
                  pl.BlockSpec((tm, POOL_WIDTH), lambda i: (i, u_col)),
                  pl.BlockSpec((POOL_HALO, POOL_WIDTH),
                               lambda i: (jnp.maximum(i * halo_blocks - 1, 0), u_col)),
                  pl.BlockSpec((tm, POOL_WIDTH), lambda i: (i, u_col + 1)),
                  pl.BlockSpec((tm, D_MODEL), lambda i: (i, 0)),
                  pl.BlockSpec(w_pool.shape, lambda i: (0, 0, 0)),
                  pl.BlockSpec((1, POOL_WIDTH), lambda i: (0, 0)),
                  pl.BlockSpec(w_out.shape, lambda i: (0, 0)),
                  pl.BlockSpec((1, D_MODEL), lambda i: (0, 0)),
                  pl.BlockSpec((1, D_MODEL), lambda i: (0, 0))],
        out_specs=pl.BlockSpec((tm, D_MODEL), lambda i: (i, 0)),
        out_shape=jax.ShapeDtypeStruct((m, D_MODEL), F32),
        compiler_params=_params("parallel"),
        name="even_tail",
    )(ma, gu, gu, gu, x, w_pool, pool_scale, w_out, ln_g, ln_b)


def _shifted_delta(x, halo_ref, first):
    last = jnp.where(first, 0.0, halo_ref[SUBLANES - 1:SUBLANES, :])
    row = lax.broadcasted_iota(jnp.int32, (x.shape[0], 1), 0)
    prev = jnp.where(row == 0, last, pltpu.roll(x, 1, 0))
    return prev - x


def _shift_proj_kernel(x_ref, halo_ref, mu_ref, w_ref, o_ref, *, tm, seq):
    i = pl.program_id(1)
    x = x_ref[...]
    xx = _shifted_delta(x, halo_ref, (i * tm) % seq == 0)
    o_ref[0] = _dot(x + xx * mu_ref[0], w_ref[0])


def _shift_project(x, mu4, w4, seq, tm):
    m = x.shape[0]
    n_proj = w4.shape[0]
    halo_blocks = tm // SUBLANES
    kernel = functools.partial(_shift_proj_kernel, tm=tm, seq=seq)
    return pl.pallas_call(
        kernel,
        grid=(n_proj, m // tm),
        in_specs=[pl.BlockSpec((tm, D_MODEL), lambda j, i: (i, 0)),
                  pl.BlockSpec((SUBLANES, D_MODEL),
                               lambda j, i: (jnp.maximum(i * halo_blocks - 1, 0), 0)),
                  pl.BlockSpec((1, 1, D_MODEL), lambda j, i: (j, 0, 0)),
                  pl.BlockSpec((1, D_MODEL, D_MODEL), lambda j, i: (j, 0, 0))],
        out_specs=pl.BlockSpec((1, tm, D_MODEL), lambda j, i: (j, i, 0)),
        out_shape=jax.ShapeDtypeStruct((n_proj, m, D_MODEL), F32),
        compiler_params=_params("parallel", "parallel"),
        name="shift_proj",
    )(x, x, mu4, w4)


def _lora_kernel(x_ref, halo_ref, mu_ref, w1_ref, w2_ref, w0_ref, a1_ref, a2_ref, a0_ref,
                 lwa_ref, *, tm, seq):
    i = pl.program_id(0)
    x = x_ref[...]
    xx = _shifted_delta(x, halo_ref, (i * tm) % seq == 0)
    xw = x + xx * mu_ref[0:1, :]
    xa = x + xx * mu_ref[1:2, :]
    ww = w0_ref[...] + _dot(jnp.tanh(_dot(xw, w1_ref[...])), w2_ref[...])
    lwa_ref[0] = -DECAY_SCALE / (1.0 + jnp.exp(-ww))
    aa = a0_ref[...] + _dot(_dot(xa, a1_ref[...]), a2_ref[...])
    lwa_ref[1] = 1.0 / (1.0 + jnp.exp(-aa))


def _lora(x, mu2, w1, w2, w0, a1, a2, a0, seq, tm):
    m = x.shape[0]
    halo_blocks = tm // SUBLANES
    kernel = functools.partial(_lora_kernel, tm=tm, seq=seq)
    full = lambda arr: pl.BlockSpec(arr.shape, lambda i: (0,) * arr.ndim)
    return pl.pallas_call(
        kernel,
        grid=(m // tm,),
        in_specs=[pl.BlockSpec((tm, D_MODEL), lambda i: (i, 0)),
                  pl.BlockSpec((SUBLANES, D_MODEL),
                               lambda i: (jnp.maximum(i * halo_blocks - 1, 0), 0)),
                  full(mu2), full(w1), full(w2), full(w0), full(a1), full(a2), full(a0)],
        out_specs=pl.BlockSpec((2, tm, D_MODEL), lambda i: (0, i, 0)),
        out_shape=jax.ShapeDtypeStruct((2, m, D_MODEL), F32),
        compiler_params=_params("parallel"),
        name="decay_iclr_lora",
    )(x, x, mu2, w1, w2, w0, a1, a2, a0)


def _blockdiag(y, bd_mask):
    tiled = jnp.concatenate([y.astype(BF16)] * WKV_HEADS, axis=0)
    return tiled * bd_mask


def _fold_diag_blocks(full, head_masks):
    c = WKV_CHUNK
    out = jnp.where(head_masks[0], full[0:c, :], 0.0)
    for h in range(1, WKV_HEADS):
        out = out + jnp.where(head_masks[h], full[h * c:(h + 1) * c, :], 0.0)
    return out


def _wkv_kernel(r_ref, k_ref, v_ref, g_ref, lw_ref, a_ref, kk_ref, ka_ref, rk_ref,
                gw_ref, gb_ref, o_ref, state_ref, *, chunks, quads):
    @pl.when(pl.program_id(2) == 0)
    def _():
        state_ref[...] = jnp.zeros_like(state_ref)

    c = WKV_CHUNK
    w = WKV_LANES
    t = chunks * c
    srow = lax.broadcasted_iota(jnp.int32, (w, w), 0)
    scol = lax.broadcasted_iota(jnp.int32, (w, w), 1)
    bd_mask = jnp.where((srow // c) == (scol // c), 1.0, 0.0).astype(BF16)
    lane = lax.broadcasted_iota(jnp.int32, (1, w), 1)
    head_masks = [(lane // c) == h for h in range(WKV_HEADS)]
    crow = lax.broadcasted_iota(jnp.int32, (c, w), 0)
    ccol = lax.broadcasted_iota(jnp.int32, (c, w), 1) % c
    strict = crow > ccol
    lower = crow >= ccol
    eye_cat = jnp.where(crow == ccol, 1.0, 0.0)
    tg = min(t, w)
    trow = lax.broadcasted_iota(jnp.int32, (tg, tg), 0)
    tcol = lax.broadcasted_iota(jnp.int32, (tg, tg), 1)
    prefix = jnp.where(jnp.logical_and((trow // c) == (tcol // c), trow >= tcol),
                       1.0, 0.0).astype(BF16)

    def prepare(qi):
        lanes = slice(qi * w, (qi + 1) * w)
        r = r_ref[0, 0, :, lanes]
        k = k_ref[0, 0, :, lanes]
        v = v_ref[0, 0, :, lanes]
        lw = lw_ref[0, 0, :, lanes]
        a = a_ref[0, 0, :, lanes]

        kk = k * kk_ref[:, lanes]
        kmod = k * (1.0 + (a - 1.0) * ka_ref[:, lanes])
        sums = _dot(jnp.concatenate([kk * kk, r * kmod * rk_ref[:, lanes]], axis=0), bd_mask)
        kk = kk / jnp.maximum(jnp.sqrt(sums[0:t]), L2_EPS)
        b = a * kk

        lw_hi = lw.astype(BF16)
        lw_lo = (lw - lw_hi.astype(F32)).astype(BF16)
        lw_split = jnp.concatenate([lw_hi, lw_lo], axis=1)
        cum2 = jnp.concatenate([jnp.dot(prefix, lw_split[s:s + tg], preferred_element_type=F32)
                                for s in range(0, t, tg)], axis=0)
        cum = cum2[:, 0:w] + cum2[:, w:2 * w]
        totals = [cum[(ci + 1) * c - 1:(ci + 1) * c, :] for ci in range(chunks)]
        total_rows = jnp.concatenate([jnp.broadcast_to(tot, (c, w)) for tot in totals], axis=0)
        g_inv = jnp.exp(-cum)
        g_tail = jnp.exp(total_rows - cum)
        whole = dict(kkg=kk * jnp.exp(cum - lw), rg=r * jnp.exp(cum), kd=kmod * g_inv,
                     bd=b * g_inv, kt=kmod * g_tail, bt=b * g_tail, v=v)
        per_chunk = [{name: x[ci * c:(ci + 1) * c] for name, x in whole.items()}
                     for ci in range(chunks)]
        for ci in range(chunks):
            per_chunk[ci]["total"] = totals[ci]
        return per_chunk, sums[t:2 * t] * v

    prepared = [prepare(qi) for qi in range(quads)]
    units = [(qi, ci) for ci in range(chunks) for qi in range(quads)]
    every = range(len(units))
    un = [prepared[qi][0][ci] for qi, ci in units]
    bdm = lambda y: _blockdiag(y, bd_mask)

    lhs = [jnp.concatenate([un[ci]["kkg"], un[ci]["rg"]], axis=0) for ci in every]
    a1 = [_dot_nt(lhs[ci], bdm(un[ci]["bd"])) for ci in every]
    a2 = [_dot_nt(lhs[ci], bdm(un[ci]["kd"])) for ci in every]
    a_kb = [jnp.where(strict, a1[ci][0:c], 0.0) for ci in every]
    a_rb = [jnp.where(lower, a1[ci][c:2 * c], 0.0) for ci in every]
    a_kk = [jnp.where(strict, a2[ci][0:c], 0.0) for ci in every]
    a_rk = [jnp.where(lower, a2[ci][c:2 * c], 0.0) for ci in every]

    power = [-a_kb[ci] for ci in every]
    inv = [eye_cat + power[ci] for ci in every]
    power = [_dot(power[ci], bdm(power[ci])) for ci in every]
    span = 2
    while 2 * span < c:
        both = [_dot(jnp.concatenate([inv[ci], power[ci]], axis=0), bdm(power[ci])) for ci in every]
        inv = [inv[ci] + both[ci][0:c] for ci in every]
        power = [both[ci][c:2 * c] for ci in every]
        span *= 2
    inv = [inv[ci] + _dot(inv[ci], bdm(power[ci])) for ci in every]

    av = [_dot(jnp.concatenate([a_kk[ci], a_rk[ci]], axis=0), bdm(un[ci]["v"])) for ci in every]
    akkv = [av[ci][0:c] for ci in every]
    arkv = [av[ci][c:2 * c] for ci in every]
    wu = [_dot(inv[ci], jnp.concatenate([bdm(un[ci]["kkg"]), bdm(akkv[ci])], axis=1)) for ci in every]
    w1 = [wu[ci][:, 0:w] for ci in every]
    u0 = [wu[ci][:, w:2 * w] for ci in every]
    arb = [_dot(a_rb[ci], jnp.concatenate([bdm(w1[ci]), bdm(u0[ci])], axis=1)) for ci in every]
    r2 = [un[ci]["rg"] - arb[ci][:, 0:w] for ci in every]
    o0 = [arkv[ci] - arb[ci][:, w:2 * w] for ci in every]
    trans = [jnp.where(crow == ccol, jnp.exp(un[ci]["total"]), 0.0)
             - _fold_diag_blocks(_dot_tn(un[ci]["bt"], w1[ci]), head_masks) for ci in every]
    fresh = [_fold_diag_blocks(
        _dot_tn(jnp.concatenate([un[ci]["kt"], -un[ci]["bt"]], axis=0),
                jnp.concatenate([un[ci]["v"], u0[ci]], axis=0)), head_masks) for ci in every]

    states = [state_ref[qi] for qi in range(quads)]
    outs = [[] for _ in range(quads)]
    for ui, (qi, _) in enumerate(units):
        both = _dot(jnp.concatenate([trans[ui], r2[ui]], axis=0), bdm(states[qi]))
        states[qi] = both[0:c] + fresh[ui]
        outs[qi].append(both[c:2 * c] + o0[ui])

    inv_n = 1.0 / RWKV_HEAD_DIM
    for qi in range(quads):
        lanes = slice(qi * w, (qi + 1) * w)
        state_ref[qi] = states[qi]
        o = jnp.concatenate(outs[qi], axis=0)
        d = o - _dot(o, bd_mask) * inv_n
        var = _dot(d * d, bd_mask) * inv_n
        normed = d * lax.rsqrt(var + GN_EPS) * gw_ref[:, lanes] + gb_ref[:, lanes]
        gate = _silu(g_ref[0, 0, :, lanes])
        o_ref[0, :, lanes] = ((normed + prepared[qi][1]) * gate).astype(o_ref.dtype)


def _wkv(rkvg, lwa, k_k, k_a, r_k, gn_w, gn_b, chunks, quads):
    _, b, s, d = rkvg.shape
    t = chunks * WKV_CHUNK
    width = quads * WKV_LANES
    kernel = functools.partial(_wkv_kernel, chunks=chunks, quads=quads)
    proj = lambda j: pl.BlockSpec((1, 1, t, width), lambda bi, h, ci: (j, bi, ci, h))
    tok = pl.BlockSpec((1, t, width), lambda bi, h, ci: (bi, ci, h))
    vec = pl.BlockSpec((1, width), lambda bi, h, ci: (0, h))
    return pl.pallas_call(
        kernel,
        grid=(b, d // width, s // t),
        in_specs=[proj(0), proj(1), proj(2), proj(3), proj(0), proj(1), vec, vec, vec, vec, vec],
        out_specs=tok,
        out_shape=jax.ShapeDtypeStruct((b, s, d), BF16),
        scratch_shapes=[pltpu.VMEM((quads, RWKV_HEAD_DIM, WKV_LANES), F32)],
        compiler_params=_params("parallel", "parallel", "arbitrary"),
        name="wkv7_chunked",
    )(rkvg, rkvg, rkvg, rkvg, lwa, lwa, k_k, k_a, r_k, gn_w, gn_b)


def _odd_tail_kernel(m_ref, x_ref, wo_ref, lg_ref, lb_ref, o_ref):
    y = jnp.dot(m_ref[...], wo_ref[...], preferred_element_type=F32)
    z = DEEPNORM_ALPHA * x_ref[...] + y
    o_ref[...] = _layer_norm_rows(z, lg_ref[...], lb_ref[...])


def _odd_tail(mixed, x, w_o, ln_g, ln_b, tm):
    m = x.shape[0]
    return pl.pallas_call(
        _odd_tail_kernel,
        grid=(m // tm,),
        in_specs=[pl.BlockSpec((tm, D_MODEL), lambda i: (i, 0)),
                  pl.BlockSpec((tm, D_MODEL), lambda i: (i, 0)),
                  pl.BlockSpec(w_o.shape, lambda i: (0, 0)),
                  pl.BlockSpec((1, D_MODEL), lambda i: (0, 0)),
                  pl.BlockSpec((1, D_MODEL), lambda i: (0, 0))],
        out_specs=pl.BlockSpec((tm, D_MODEL), lambda i: (i, 0)),
        out_shape=jax.ShapeDtypeStruct((m, D_MODEL), F32),
        compiler_params=_params("parallel"),
        name="odd_tail",
    )(mixed, x, w_o, ln_g, ln_b)


def _even_layer(x2, seq, w_in, w_pool, pool_scale, w_out, ln_g, ln_b):
    m = x2.shape[0]
    batch = m // seq
    split = 3 * SB_WIDTH
    w_in = w_in.astype(BF16)
    q_scale = jnp.concatenate([jnp.full((1, SB_WIDTH), SB_HEAD_DIM ** -0.5, F32),
                               jnp.ones((1, split - SB_WIDTH), F32)], axis=1)
    qkv = _project(x2, w_in[:, :split], q_scale, BF16, tm=1024, tn=1024)
    gu = _project(x2, w_in[:, split:], jnp.ones((1, split), F32), F32, tm=1024, tn=1024)
    ma = _attention(qkv.reshape(batch, seq, split), gu.reshape(batch, seq, split),
                    tq=512, tk=256, heads=4)
    return _even_tail(ma.reshape(m, SB_WIDTH), gu, x2, w_pool.astype(BF16),
                      pool_scale.reshape(1, POOL_WIDTH), w_out.astype(BF16),
                      ln_g.reshape(1, D_MODEL), ln_b.reshape(1, D_MODEL), seq, tm=512)


def _pad_rank(w, axis):
    pad = [(0, 0), (0, 0)]
    pad[axis] = (0, LORA_PAD - w.shape[axis])
    return jnp.pad(w, pad).astype(BF16)


def _odd_layer(x2, seq, mu, w_r, w_k, w_v, w_g, w0, w1, w2, a0, a1, a2, k_k, k_a, r_k,
               gn_w, gn_b, w_o, ln_g, ln_b):
    m = x2.shape[0]
    batch = m // seq
    row = lambda vec: vec.reshape(1, D_MODEL)
    mu4 = jnp.stack([mu[0], mu[2], mu[3], mu[5]]).reshape(4, 1, D_MODEL)
    w4 = jnp.stack([w_r, w_k, w_v, w_g]).astype(BF16)
    rkvg = _shift_project(x2, mu4, w4, seq, tm=512)
    mu2 = jnp.stack([mu[1], mu[4]])
    lwa = _lora(x2, mu2, _pad_rank(w1, 1), _pad_rank(w2, 0), row(w0),
                _pad_rank(a1, 1), _pad_rank(a2, 0), row(a0), seq, tm=256)
    mixed = _wkv(rkvg.reshape(4, batch, seq, D_MODEL), lwa.reshape(2, batch, seq, D_MODEL),
                 row(k_k), row(k_a), row(r_k),
                 row(gn_w), row(gn_b), chunks=4, quads=4)
    return _odd_tail(mixed.reshape(m, D_MODEL), x2, w_o.astype(BF16), row(ln_g), row(ln_b), tm=512)


def kernel(x, ev_w_in, ev_w_pool, ev_pool_scale, ev_w_out, od_mu, od_w_r, od_w_k, od_w_v, od_w_g,
           od_w0, od_w1, od_w2, od_a0, od_a1, od_a2, od_k_k, od_k_a, od_r_k, od_gn_w, od_gn_b,
           od_w_o, ln_g, ln_b):
    batch, seq, d = x.shape
    x2 = x.reshape(batch * seq, d)
    for layer in range(N_LAYERS):
        j = layer // 2
        if layer % 2 == 0:
            x2 = _even_layer(x2, seq, ev_w_in[j], ev_w_pool[j], ev_pool_scale[j], ev_w_out[j],
                             ln_g[layer], ln_b[layer])
        else:
            x2 = _odd_layer(x2, seq, od_mu[j], od_w_r[j], od_w_k[j], od_w_v[j], od_w_g[j],
                            od_w0[j], od_w1[j], od_w2[j], od_a0[j], od_a1[j], od_a2[j],
                            od_k_k[j], od_k_a[j], od_r_k[j], od_gn_w[j], od_gn_b[j], od_w_o[j],
                            ln_g[layer], ln_b[layer])
    return x2.reshape(batch, seq, d)
```

```python
import functools
import math

import jax
import jax.numpy as jnp
from jax import lax
from jax.experimental import pallas as pl
from jax.experimental.pallas import tpu as pltpu

F32 = jnp.float32
BF16 = jnp.bfloat16

D_MODEL = 2048
N_LAYERS = 2
SB_HEADS = 8
SB_HEAD_DIM = 128
SB_WIDTH = SB_HEADS * SB_HEAD_DIM
POOL_WINDOWS = (2, 4, 8, 16)
POOL_GROUP_DIM = 256
POOL_WIDTH = len(POOL_WINDOWS) * POOL_GROUP_DIM
POOL_HALO = 16
RWKV_HEAD_DIM = 64
LORA_PAD = 128
GN_EPS = 64e-5
L2_EPS = 1e-12
LN_EPS = 1e-5
DEEPNORM_ALPHA = (2 * N_LAYERS) ** 0.25

SUBLANES = 8
LANES = 128
VMEM_LIMIT = 52 * 1024 * 1024

EXP_UNDERFLOW = -104.0
DECAY_SCALE = math.exp(-0.5)

WKV_CHUNK = 64
WKV_HEADS = 4
WKV_LANES = WKV_HEADS * RWKV_HEAD_DIM
assert WKV_CHUNK == RWKV_HEAD_DIM


def _dot(a, b):
    return jnp.dot(a.astype(BF16), b.astype(BF16), preferred_element_type=F32)


def _dot_nt(a, b):
    return lax.dot_general(a.astype(BF16), b.astype(BF16), (((1,), (1,)), ((), ())),
                           preferred_element_type=F32)


def _dot_tn(a, b):
    return lax.dot_general(a.astype(BF16), b.astype(BF16), (((0,), (0,)), ((), ())),
                           preferred_element_type=F32)


def _silu(g):
    return g / (1.0 + jnp.exp(-g))


def _layer_norm_rows(z, g, b):
    mean = jnp.mean(z, axis=-1, keepdims=True)
    d = z - mean
    var = jnp.mean(d * d, axis=-1, keepdims=True)
    return d * lax.rsqrt(var + LN_EPS) * g + b


def _params(*sem):
    return pltpu.CompilerParams(dimension_semantics=sem, vmem_limit_bytes=VMEM_LIMIT)


def _proj_kernel(x_ref, w_ref, s_ref, o_ref, xb_ref):
    @pl.when(pl.program_id(1) == 0)
    def _():
        xb_ref[...] = x_ref[...].astype(BF16)

    acc = jnp.dot(xb_ref[...], w_ref[...], preferred_element_type=F32)
    o_ref[...] = (acc * s_ref[...]).astype(o_ref.dtype)


def _project(x, w, first_col, col_scale, out_dtype, tm, tn):
    m, k = x.shape
    n = col_scale.shape[1]
    skip = first_col // tn
    return pl.pallas_call(
        _proj_kernel,
        grid=(m // tm, n // tn),
        in_specs=[pl.BlockSpec((tm, k), lambda i, j: (i, 0)),
                  pl.BlockSpec((k, tn), lambda i, j: (0, skip + j)),
                  pl.BlockSpec((1, tn), lambda i, j: (0, j))],
        out_specs=pl.BlockSpec((tm, tn), lambda i, j: (i, j)),
        out_shape=jax.ShapeDtypeStruct((m, n), out_dtype),
        scratch_shapes=[pltpu.VMEM((tm, k), BF16)],
        compiler_params=_params("parallel", "arbitrary"),
        name="in_proj",
    )(x, w, col_scale)


def _attn_kernel(q_ref, k_ref, v_ref, g_ref, o_ref, *, tq, tk, heads):
    i = pl.program_id(2)
    inner = tq // tk
    krow = lax.broadcasted_iota(jnp.int32, (tk, tk), 0)
    kcol = lax.broadcasted_iota(jnp.int32, (tk, tk), 1)
    after = jnp.where(krow > kcol, 1.0, 0.0).astype(BF16)
    lanes = [slice(h * SB_HEAD_DIM, (h + 1) * SB_HEAD_DIM) for h in range(heads)]

    def block(h, rows, start, carry, acc, diagonal):
        q = q_ref[0, rows, lanes[h]]
        kb = k_ref[0, pl.ds(start, tk), lanes[h]]
        vb = v_ref[0, pl.ds(start, tk), lanes[h]]
        z = _dot_nt(q, kb)
        log_beta = jnp.minimum(z, 0.0) - jnp.log(1.0 + jnp.exp(-jnp.abs(z)))
        log_keep = log_beta - z
        if diagonal:
            causal = (lax.broadcasted_iota(jnp.int32, z.shape, 1)
                      < lax.broadcasted_iota(jnp.int32, z.shape, 0))
            log_keep = jnp.where(causal, log_keep, 0.0)
        later = jnp.dot(log_keep.astype(BF16), after, preferred_element_type=F32) + carry
        w = jnp.exp(log_beta + later)
        if diagonal:
            w = jnp.where(causal, w, 0.0)
        acc = acc + _dot(w, vb)
        carry = carry + jnp.sum(log_keep, axis=-1, keepdims=True)
        return carry, acc

    carry = [jnp.zeros((tq, 1), F32) for _ in range(heads)]
    acc = [jnp.zeros((tq, SB_HEAD_DIM), F32) for _ in range(heads)]
    for jo in reversed(range(inner)):
        lo_row = jo * tk
        start = pl.multiple_of(i * tq + lo_row, tk)
        for h in range(heads):
            c_new, a_new = block(h, slice(lo_row, tq), start, carry[h][lo_row:tq],
                                 acc[h][lo_row:tq], True)
            carry[h] = jnp.concatenate([carry[h][0:lo_row], c_new], axis=0) if lo_row else c_new
            acc[h] = jnp.concatenate([acc[h][0:lo_row], a_new], axis=0) if lo_row else a_new

    def cond(state):
        j, carries, _ = state
        worst = functools.reduce(jnp.maximum, carries)
        return jnp.logical_and(j >= 0, jnp.max(worst) > EXP_UNDERFLOW)

    for lo_row in range(0, tq, tk):
        rows = slice(lo_row, lo_row + tk)

        def body(state, rows=rows):
            j, carries, accs = state
            start = pl.multiple_of(j * tk, tk)
            done = [block(h, rows, start, carries[h], accs[h], False) for h in range(heads)]
            return j - 1, tuple(c for c, _ in done), tuple(a for _, a in done)

        first = (i * inner - 1, tuple(c[rows] for c in carry), tuple(a[rows] for a in acc))
        _, _, parts = lax.while_loop(cond, body, first)
        for h in range(heads):
            gate = _silu(g_ref[0, rows, lanes[h]])
            o_ref[0, rows, lanes[h]] = (parts[h] * gate).astype(o_ref.dtype)


def _attention(qkv, gu, tq, tk, heads):
    b, s, _ = qkv.shape
    width = heads * SB_HEAD_DIM
    groups = SB_HEADS // heads
    kernel = functools.partial(_attn_kernel, tq=tq, tk=tk, heads=heads)
    return pl.pallas_call(
        kernel,
        grid=(b, groups, s // tq),
        in_specs=[pl.BlockSpec((1, tq, width), lambda bi, h, i: (bi, i, h)),
                  pl.BlockSpec((1, s, width), lambda bi, h, i: (bi, 0, groups + h)),
                  pl.BlockSpec((1, s, width), lambda bi, h, i: (bi, 0, 2 * groups + h)),
                  pl.BlockSpec((1, tq, width), lambda bi, h, i: (bi, i, h))],
        out_specs=pl.BlockSpec((1, tq, width), lambda bi, h, i: (bi, i, h)),
        out_shape=jax.ShapeDtypeStruct((b, s, SB_WIDTH), BF16),
        compiler_params=_params("parallel", "parallel", "arbitrary"),
        name="stickbreak_attn",
    )(qkv, qkv, qkv, gu)


def _even_tail_kernel(ma_ref, u_ref, halo_ref, gb_ref, x_ref, wp_ref, ps_ref, wo_ref,
                      lg_ref, lb_ref, o_ref, *, tm, seq):
    i = pl.program_id(0)
    first = (i * tm) % seq == 0
    u = u_ref[...]
    halo = jnp.where(first, 0.0, halo_ref[...])
    ext = jnp.concatenate([halo, u], axis=0)
    pos = (i * tm) % seq + lax.broadcasted_iota(jnp.int32, (tm, 1), 0)

    y = jnp.dot(ma_ref[...], wo_ref[0:SB_WIDTH, :], preferred_element_type=F32)
    for g, window in enumerate(POOL_WINDOWS):
        lo = g * POOL_GROUP_DIM
        hi = lo + POOL_GROUP_DIM
        acc = ext[:, lo:hi]
        span = 1
        while span < window:
            acc = acc + pltpu.roll(acc, span, 0)
            span *= 2
        count = jnp.minimum(pos + 1, window).astype(F32)
        pooled = acc[POOL_HALO:, :] / count - u[:, lo:hi]
        mixed = _dot(pooled, wp_ref[g]) * ps_ref[:, lo:hi]
        mb = mixed * _silu(gb_ref[:, lo:hi])
        y = y + _dot(mb, wo_ref[SB_WIDTH + lo:SB_WIDTH + hi, :])

    z = DEEPNORM_ALPHA * x_ref[...] + y
    o_ref[...] = _layer_norm_rows(z, lg_ref[...], lb_ref[...])


def _even_tail(ma, gu, x, w_pool, pool_scale, w_out, ln_g, ln_b, seq, tm):
    m = x.shape[0]
    halo_blocks = tm // POOL_HALO
    u_col = SB_WIDTH // POOL_WIDTH
    kernel = functools.partial(_even_tail_kernel, tm=tm, seq=seq)
    return pl.pallas_call(
        kernel,
        grid=(m // tm,),
        in_specs=[pl.BlockSpec((tm, SB_WIDTH), lambda i: (i, 0)),
                  pl.BlockSpec((tm, POOL_WIDTH), lambda i: (i, u_col)),
                  pl.BlockSpec((POOL_HALO, POOL_WIDTH),
                               lambda i: (jnp.maximum(i * halo_blocks - 1, 0), u_col)),
                  pl.BlockSpec((tm, POOL_WIDTH), lambda i: (i, u_col + 1)),
                  pl.BlockSpec((tm, D_MODEL), lambda i: (i, 0)),
                  pl.BlockSpec(w_pool.shape, lambda i: (0, 0, 0)),
                  pl.BlockSpec((1, POOL_WIDTH), lambda i: (0, 0)),
                  pl.BlockSpec(w_out.shape, lambda i: (0, 0)),
                  pl.BlockSpec((1, D_MODEL), lambda i: (0, 0)),
                  pl.BlockSpec((1, D_MODEL), lambda i: (0, 0))],
        out_specs=pl.BlockSpec((tm, D_MODEL), lambda i: (i, 0)),
        out_shape=jax.ShapeDtypeStruct((m, D_MODEL), F32),
        compiler_params=_params("parallel"),
        name="even_tail",
    )(ma, gu, gu, gu, x, w_pool, pool_scale, w_out, ln_g, ln_b)


def _shifted_delta(x, halo_ref, first):
    last = jnp.where(first, 0.0, halo_ref[SUBLANES - 1:SUBLANES, :])
    row = lax.broadcasted_iota(jnp.int32, (x.shape[0], 1), 0)
    prev = jnp.where(row == 0, last, pltpu.roll(x, 1, 0))
    return prev - x


def _shift_proj_kernel(x_ref, halo_ref, mu_ref, w_ref, o_ref, *, tm, seq):
    i = pl.program_id(0)
    x = x_ref[...]
    xx = _shifted_delta(x, halo_ref, (i * tm) % seq == 0)
    o_ref[...] = _dot(x + xx * mu_ref[...], w_ref[...])


def _shift_project(x, mu, w, seq, tm):
    m = x.shape[0]
    halo_blocks = tm // SUBLANES
    kernel = functools.partial(_shift_proj_kernel, tm=tm, seq=seq)
    return pl.pallas_call(
        kernel,
        grid=(m // tm,),
        in_specs=[pl.BlockSpec((tm, D_MODEL), lambda i: (i, 0)),
                  pl.BlockSpec((SUBLANES, D_MODEL),
                               lambda i: (jnp.maximum(i * halo_blocks - 1, 0), 0)),
                  pl.BlockSpec((1, D_MODEL), lambda i: (0, 0)),
                  pl.BlockSpec((D_MODEL, D_MODEL), lambda i: (0, 0))],
        out_specs=pl.BlockSpec((tm, D_MODEL), lambda i: (i, 0)),
        out_shape=jax.ShapeDtypeStruct((m, D_MODEL), F32),
        compiler_params=_params("parallel"),
        name="shift_proj",
    )(x, x, mu, w)


def _lora_kernel(x_ref, halo_ref, mu_ref, w1_ref, w2_ref, w0_ref, a1_ref, a2_ref, a0_ref,
                 lw_ref, a_ref, *, tm, seq):
    i = pl.program_id(0)
    x = x_ref[...]
    xx = _shifted_delta(x, halo_ref, (i * tm) % seq == 0)
    xw = x + xx * mu_ref[0:1, :]
    xa = x + xx * mu_ref[1:2, :]
    ww = w0_ref[...] + _dot(jnp.tanh(_dot(xw, w1_ref[...])), w2_ref[...])
    lw_ref[...] = -DECAY_SCALE / (1.0 + jnp.exp(-ww))
    aa = a0_ref[...] + _dot(_dot(xa, a1_ref[...]), a2_ref[...])
    a_ref[...] = 1.0 / (1.0 + jnp.exp(-aa))


def _lora(x, mu2, w1, w2, w0, a1, a2, a0, seq, tm):
    m = x.shape[0]
    halo_blocks = tm // SUBLANES
    kernel = functools.partial(_lora_kernel, tm=tm, seq=seq)
    full = lambda arr: pl.BlockSpec(arr.shape, lambda i: (0,) * arr.ndim)
    out = jax.ShapeDtypeStruct((m, D_MODEL), F32)
    return pl.pallas_call(
        kernel,
        grid=(m // tm,),
        in_specs=[pl.BlockSpec((tm, D_MODEL), lambda i: (i, 0)),
                  pl.BlockSpec((SUBLANES, D_MODEL),
                               lambda i: (jnp.maximum(i * halo_blocks - 1, 0), 0)),
                  full(mu2), full(w1), full(w2), full(w0), full(a1), full(a2), full(a0)],
        out_specs=[pl.BlockSpec((tm, D_MODEL), lambda i: (i, 0)),
                   pl.BlockSpec((tm, D_MODEL), lambda i: (i, 0))],
        out_shape=[out, out],
        compiler_params=_params("parallel"),
        name="decay_iclr_lora",
    )(x, x, mu2, w1, w2, w0, a1, a2, a0)


def _blockdiag(y, bd_mask):
    tiled = jnp.concatenate([y.astype(BF16)] * WKV_HEADS, axis=0)
    return tiled * bd_mask


def _fold_diag_blocks(full, head_masks):
    c = WKV_CHUNK
    out = jnp.where(head_masks[0], full[0:c, :], 0.0)
    for h in range(1, WKV_HEADS):
        out = out + jnp.where(head_masks[h], full[h * c:(h + 1) * c, :], 0.0)
    return out


def _wkv_kernel(r_ref, k_ref, v_ref, g_ref, lw_ref, a_ref, kk_ref, ka_ref, rk_ref,
                gw_ref, gb_ref, o_ref, state_ref, *, chunks, quads):
    @pl.when(pl.program_id(2) == 0)
    def _():
        state_ref[...] = jnp.zeros_like(state_ref)

    c = WKV_CHUNK
    w = WKV_LANES
    t = chunks * c
    srow = lax.broadcasted_iota(jnp.int32, (w, w), 0)
    scol = lax.broadcasted_iota(jnp.int32, (w, w), 1)
    bd_mask = jnp.where((srow // c) == (scol // c), 1.0, 0.0).astype(BF16)
    lane = lax.broadcasted_iota(jnp.int32, (1, w), 1)
    head_masks = [(lane // c) == h for h in range(WKV_HEADS)]
    crow = lax.broadcasted_iota(jnp.int32, (c, w), 0)
    ccol = lax.broadcasted_iota(jnp.int32, (c, w), 1) % c
    strict = crow > ccol
    lower = crow >= ccol
    eye_cat = jnp.where(crow == ccol, 1.0, 0.0)
    tg = min(t, w)
    trow = lax.broadcasted_iota(jnp.int32, (tg, tg), 0)
    tcol = lax.broadcasted_iota(jnp.int32, (tg, tg), 1)
    prefix = jnp.where(jnp.logical_and((trow // c) == (tcol // c), trow >= tcol),
                       1.0, 0.0).astype(BF16)

    def prepare(qi):
        lanes = slice(qi * w, (qi + 1) * w)
        r = r_ref[0, :, lanes]
        k = k_ref[0, :, lanes]
        v = v_ref[0, :, lanes]
        lw = lw_ref[0, :, lanes]
        a = a_ref[0, :, lanes]

        kk = k * kk_ref[:, lanes]
        kmod = k * (1.0 + (a - 1.0) * ka_ref[:, lanes])
        sums = _dot(jnp.concatenate([kk * kk, r * kmod * rk_ref[:, lanes]], axis=0), bd_mask)
        kk = kk / jnp.maximum(jnp.sqrt(sums[0:t]), L2_EPS)
        b = a * kk

        lw_hi = lw.astype(BF16)
        lw_lo = (lw - lw_hi.astype(F32)).astype(BF16)
        lw_split = jnp.concatenate([lw_hi, lw_lo], axis=1)
        cum2 = jnp.concatenate([jnp.dot(prefix, lw_split[s:s + tg], preferred_element_type=F32)
                                for s in range(0, t, tg)], axis=0)
        cum = cum2[:, 0:w] + cum2[:, w:2 * w]
        totals = [cum[(ci + 1) * c - 1:(ci + 1) * c, :] for ci in range(chunks)]
        total_rows = jnp.concatenate([jnp.broadcast_to(tot, (c, w)) for tot in totals], axis=0)
        g_inv = jnp.exp(-cum)
        g_tail = jnp.exp(total_rows - cum)
        whole = dict(kkg=kk * jnp.exp(cum - lw), rg=r * jnp.exp(cum), kd=kmod * g_inv,
                     bd=b * g_inv, kt=kmod * g_tail, bt=b * g_tail, v=v)
        per_chunk = [{name: x[ci * c:(ci + 1) * c] for name, x in whole.items()}
                     for ci in range(chunks)]
        for ci in range(chunks):
            per_chunk[ci]["total"] = totals[ci]
        return per_chunk, sums[t:2 * t] * v

    prepared = [prepare(qi) for qi in range(quads)]
    units = [(qi, ci) for ci in range(chunks) for qi in range(quads)]
    every = range(len(units))
    un = [prepared[qi][0][ci] for qi, ci in units]
    bdm = lambda y: _blockdiag(y, bd_mask)

    lhs = [jnp.concatenate([un[ci]["kkg"], un[ci]["rg"]], axis=0) for ci in every]
    a1 = [_dot_nt(lhs[ci], bdm(un[ci]["bd"])) for ci in every]
    a2 = [_dot_nt(lhs[ci], bdm(un[ci]["kd"])) for ci in every]
    a_kb = [jnp.where(strict, a1[ci][0:c], 0.0) for ci in every]
    a_rb = [jnp.where(lower, a1[ci][c:2 * c], 0.0) for ci in every]
    a_kk = [jnp.where(strict, a2[ci][0:c], 0.0) for ci in every]
    a_rk = [jnp.where(lower, a2[ci][c:2 * c], 0.0) for ci in every]

    power = [-a_kb[ci] for ci in every]
    inv = [eye_cat + power[ci] for ci in every]
    power = [_dot(power[ci], bdm(power[ci])) for ci in every]
    span = 2
    while 2 * span < c:
        both = [_dot(jnp.concatenate([inv[ci], power[ci]], axis=0), bdm(power[ci])) for ci in every]
        inv = [inv[ci] + both[ci][0:c] for ci in every]
        power = [both[ci][c:2 * c] for ci in every]
        span *= 2
    inv = [inv[ci] + _dot(inv[ci], bdm(power[ci])) for ci in every]

    av = [_dot(jnp.concatenate([a_kk[ci], a_rk[ci]], axis=0), bdm(un[ci]["v"])) for ci in every]
    akkv = [av[ci][0:c] for ci in every]
    arkv = [av[ci][c:2 * c] for ci in every]
    wu = [_dot(inv[ci], jnp.concatenate([bdm(un[ci]["kkg"]), bdm(akkv[ci])], axis=1)) for ci in every]
    w1 = [wu[ci][:, 0:w] for ci in every]
    u0 = [wu[ci][:, w:2 * w] for ci in every]
    arb = [_dot(a_rb[ci], jnp.concatenate([bdm(w1[ci]), bdm(u0[ci])], axis=1)) for ci in every]
    r2 = [un[ci]["rg"] - arb[ci][:, 0:w] for ci in every]
    o0 = [arkv[ci] - arb[ci][:, w:2 * w] for ci in every]
    trans = [jnp.where(crow == ccol, jnp.exp(un[ci]["total"]), 0.0)
             - _fold_diag_blocks(_dot_tn(un[ci]["bt"], w1[ci]), head_masks) for ci in every]
    fresh = [_fold_diag_blocks(
        _dot_tn(jnp.concatenate([un[ci]["kt"], -un[ci]["bt"]], axis=0),
                jnp.concatenate([un[ci]["v"], u0[ci]], axis=0)), head_masks) for ci in every]

    states = [state_ref[qi] for qi in range(quads)]
    outs = [[] for _ in range(quads)]
    for ui, (qi, _) in enumerate(units):
        both = _dot(jnp.concatenate([trans[ui], r2[ui]], axis=0), bdm(states[qi]))
        states[qi] = both[0:c] + fresh[ui]
        outs[qi].append(both[c:2 * c] + o0[ui])

    inv_n = 1.0 / RWKV_HEAD_DIM
    for qi in range(quads):
        lanes = slice(qi * w, (qi + 1) * w)
        state_ref[qi] = states[qi]
        o = jnp.concatenate(outs[qi], axis=0)
        d = o - _dot(o, bd_mask) * inv_n
        var = _dot(d * d, bd_mask) * inv_n
        normed = d * lax.rsqrt(var + GN_EPS) * gw_ref[:, lanes] + gb_ref[:, lanes]
        gate = _silu(g_ref[0, :, lanes])
        o_ref[0, :, lanes] = ((normed + prepared[qi][1]) * gate).astype(o_ref.dtype)


def _wkv(r, k, v, g, lw, a, k_k, k_a, r_k, gn_w, gn_b, chunks, quads):
    b, s, d = r.shape
    t = chunks * WKV_CHUNK
    width = quads * WKV_LANES
    kernel = functools.partial(_wkv_kernel, chunks=chunks, quads=quads)
    tok = pl.BlockSpec((1, t, width), lambda bi, h, ci: (bi, ci, h))
    vec = pl.BlockSpec((1, width), lambda bi, h, ci: (0, h))
    return pl.pallas_call(
        kernel,
        grid=(b, d // width, s // t),
        in_specs=[tok, tok, tok, tok, tok, tok, vec, vec, vec, vec, vec],
        out_specs=tok,
        out_shape=jax.ShapeDtypeStruct((b, s, d), BF16),
        scratch_shapes=[pltpu.VMEM((quads, RWKV_HEAD_DIM, WKV_LANES), F32)],
        compiler_params=_params("parallel", "parallel", "arbitrary"),
        name="wkv7_chunked",
    )(r, k, v, g, lw, a, k_k, k_a, r_k, gn_w, gn_b)


def _odd_tail_kernel(m_ref, x_ref, wo_ref, lg_ref, lb_ref, o_ref, *, parts):
    rows_per = m_ref.shape[0] // parts
    for p in range(parts):
        rows = slice(p * rows_per, (p + 1) * rows_per)
        y = jnp.dot(m_ref[rows, :], wo_ref[...], preferred_element_type=F32)
        z = DEEPNORM_ALPHA * x_ref[rows, :] + y
        o_ref[rows, :] = _layer_norm_rows(z, lg_ref[...], lb_ref[...])


def _odd_tail(mixed, x, w_o, ln_g, ln_b, tm):
    m = x.shape[0]
    return pl.pallas_call(
        functools.partial(_odd_tail_kernel, parts=4),
        grid=(m // tm,),
        in_specs=[pl.BlockSpec((tm, D_MODEL), lambda i: (i, 0)),
                  pl.BlockSpec((tm, D_MODEL), lambda i: (i, 0)),
                  pl.BlockSpec(w_o.shape, lambda i: (0, 0)),
                  pl.BlockSpec((1, D_MODEL), lambda i: (0, 0)),
                  pl.BlockSpec((1, D_MODEL), lambda i: (0, 0))],
        out_specs=pl.BlockSpec((tm, D_MODEL), lambda i: (i, 0)),
        out_shape=jax.ShapeDtypeStruct((m, D_MODEL), F32),
        compiler_params=_params("parallel"),
        name="odd_tail",
    )(mixed, x, w_o, ln_g, ln_b)


def _even_layer(x2, seq, w_in, w_pool, pool_scale, w_out, ln_g, ln_b):
    m = x2.shape[0]
    batch = m // seq
    split = 3 * SB_WIDTH
    w_in = w_in.astype(BF16)
    q_scale = jnp.concatenate([jnp.full((1, SB_WIDTH), SB_HEAD_DIM ** -0.5, F32),
                               jnp.ones((1, split - SB_WIDTH), F32)], axis=1)
    qkv = _project(x2, w_in, 0, q_scale, BF16, tm=1024, tn=1024)
    gu = _project(x2, w_in, split, jnp.ones((1, split), F32), F32, tm=1024, tn=1024)
    ma = _attention(qkv.reshape(batch, seq, split), gu.reshape(batch, seq, split),
                    tq=512, tk=256, heads=4)
    return _even_tail(ma.reshape(m, SB_WIDTH), gu, x2, w_pool.astype(BF16),
                      pool_scale.reshape(1, POOL_WIDTH), w_out.astype(BF16),
                      ln_g.reshape(1, D_MODEL), ln_b.reshape(1, D_MODEL), seq, tm=512)


def _pad_rank(w, axis):
    pad = [(0, 0), (0, 0)]
    pad[axis] = (0, LORA_PAD - w.shape[axis])
    return jnp.pad(w, pad).astype(BF16)


def _odd_layer(x2, seq, mu, w_r, w_k, w_v, w_g, w0, w1, w2, a0, a1, a2, k_k, k_a, r_k,
               gn_w, gn_b, w_o, ln_g, ln_b):
    m = x2.shape[0]
    batch = m // seq
    row = lambda vec: vec.reshape(1, D_MODEL)
    r, k, v, g = (_shift_project(x2, row(mu[j]), w.astype(BF16), seq, tm=512)
                  for j, w in ((0, w_r), (2, w_k), (3, w_v), (5, w_g)))
    mu2 = jnp.stack([mu[1], mu[4]])
    lw, a = _lora(x2, mu2, _pad_rank(w1, 1), _pad_rank(w2, 0), row(w0),
                  _pad_rank(a1, 1), _pad_rank(a2, 0), row(a0), seq, tm=256)
    tokens = lambda t2: t2.reshape(batch, seq, D_MODEL)
    mixed = _wkv(tokens(r), tokens(k), tokens(v), tokens(g), tokens(lw), tokens(a),
                 row(k_k), row(k_a), row(r_k), row(gn_w), row(gn_b), chunks=4, quads=4)
    return _odd_tail(mixed.reshape(m, D_MODEL), x2, w_o.astype(BF16), row(ln_g), row(ln_b), tm=512)


def kernel(x, ev_w_in, ev_w_pool, ev_pool_scale, ev_w_out, od_mu, od_w_r, od_w_k, od_w_v, od_w_g,
           od_w0, od_w1, od_w2, od_a0, od_a1, od_a2, od_k_k, od_k_a, od_r_k, od_gn_w, od_gn_b,
           od_w_o, ln_g, ln_b):
    batch, seq, d = x.shape
    x2 = x.reshape(batch * seq, d)
    for layer in range(N_LAYERS):
        j = layer // 2
        if layer % 2 == 0:
            x2 = _even_layer(x2, seq, ev_w_in[j], ev_w_pool[j], ev_pool_scale[j], ev_w_out[j],
                             ln_g[layer], ln_b[layer])
        else:
            x2 = _odd_layer(x2, seq, od_mu[j], od_w_r[j], od_w_k[j], od_w_v[j], od_w_g[j],
                            od_w0[j], od_w1[j], od_w2[j], od_a0[j], od_a1[j], od_a2[j],
                            od_k_k[j], od_k_a[j], od_r_k[j], od_gn_w[j], od_gn_b[j], od_w_o[j],
                            ln_g[layer], ln_b[layer])
    return x2.reshape(batch, seq, d)
```

```python
import functools
import math

import jax
import jax.numpy as jnp
from jax import lax
from jax.experimental import pallas as pl
from jax.experimental.pallas import tpu as pltpu

F32 = jnp.float32
BF16 = jnp.bfloat16

D_MODEL = 2048
N_LAYERS = 2
SB_HEADS = 8
SB_HEAD_DIM = 128
SB_WIDTH = SB_HEADS * SB_HEAD_DIM
POOL_WINDOWS = (2, 4, 8, 16)
POOL_GROUP_DIM = 256
POOL_WIDTH = len(POOL_WINDOWS) * POOL_GROUP_DIM
POOL_HALO = 16
RWKV_HEAD_DIM = 64
LORA_PAD = 128
GN_EPS = 64e-5
L2_EPS = 1e-12
LN_EPS = 1e-5
DEEPNORM_ALPHA = (2 * N_LAYERS) ** 0.25

SUBLANES = 8
LANES = 128
VMEM_LIMIT = 52 * 1024 * 1024

EXP_UNDERFLOW = -104.0
DECAY_SCALE = math.exp(-0.5)

WKV_CHUNK = 64
WKV_HEADS = 4
WKV_LANES = WKV_HEADS * RWKV_HEAD_DIM
assert WKV_CHUNK == RWKV_HEAD_DIM


def _dot(a, b):
    return jnp.dot(a.astype(BF16), b.astype(BF16), preferred_element_type=F32)


def _dot_nt(a, b):
    return lax.dot_general(a.astype(BF16), b.astype(BF16), (((1,), (1,)), ((), ())),
                           preferred_element_type=F32)


def _dot_tn(a, b):
    return lax.dot_general(a.astype(BF16), b.astype(BF16), (((0,), (0,)), ((), ())),
                           preferred_element_type=F32)


def _silu(g):
    return g / (1.0 + jnp.exp(-g))


def _layer_norm_rows(z, g, b):
    mean = jnp.mean(z, axis=-1, keepdims=True)
    d = z - mean
    var = jnp.mean(d * d, axis=-1, keepdims=True)
    return d * lax.rsqrt(var + LN_EPS) * g + b


def _params(*sem):
    return pltpu.CompilerParams(dimension_semantics=sem, vmem_limit_bytes=VMEM_LIMIT)


def _proj_kernel(x_ref, w_ref, s_ref, o_ref, xb_ref):
    @pl.when(pl.program_id(1) == 0)
    def _():
        xb_ref[...] = x_ref[...].astype(BF16)

    acc = jnp.dot(xb_ref[...], w_ref[...], preferred_element_type=F32)
    o_ref[...] = (acc * s_ref[...]).astype(o_ref.dtype)


def _project(x, w, first_col, col_scale, out_dtype, tm, tn):
    m, k = x.shape
    n = col_scale.shape[1]
    skip = first_col // tn
    return pl.pallas_call(
        _proj_kernel,
        grid=(m // tm, n // tn),
        in_specs=[pl.BlockSpec((tm, k), lambda i, j: (i, 0)),
                  pl.BlockSpec((k, tn), lambda i, j: (0, skip + j)),
                  pl.BlockSpec((1, tn), lambda i, j: (0, j))],
        out_specs=pl.BlockSpec((tm, tn), lambda i, j: (i, j)),
        out_shape=jax.ShapeDtypeStruct((m, n), out_dtype),
        scratch_shapes=[pltpu.VMEM((tm, k), BF16)],
        compiler_params=_params("parallel", "arbitrary"),
        name="in_proj",
    )(x, w, col_scale)


def _attn_kernel(q_ref, k_ref, v_ref, g_ref, o_ref, *, tq, tk, heads):
    i = pl.program_id(2)
    inner = tq // tk
    krow = lax.broadcasted_iota(jnp.int32, (tk, tk), 0)
    kcol = lax.broadcasted_iota(jnp.int32, (tk, tk), 1)
    after = jnp.where(krow > kcol, 1.0, 0.0).astype(BF16)
    lanes = [slice(h * SB_HEAD_DIM, (h + 1) * SB_HEAD_DIM) for h in range(heads)]

    def block(h, rows, start, carry, acc, diagonal):
        q = q_ref[0, rows, lanes[h]]
        kb = k_ref[0, pl.ds(start, tk), lanes[h]]
        vb = v_ref[0, pl.ds(start, tk), lanes[h]]
        z = _dot_nt(q, kb)
        log_beta = jnp.minimum(z, 0.0) - jnp.log(1.0 + jnp.exp(-jnp.abs(z)))
        log_keep = log_beta - z
        if diagonal:
            causal = (lax.broadcasted_iota(jnp.int32, z.shape, 1)
                      < lax.broadcasted_iota(jnp.int32, z.shape, 0))
            log_keep = jnp.where(causal, log_keep, 0.0)
        later = jnp.dot(log_keep.astype(BF16), after, preferred_element_type=F32) + carry
        w = jnp.exp(log_beta + later)
        if diagonal:
            w = jnp.where(causal, w, 0.0)
        acc = acc + _dot(w, vb)
        carry = carry + jnp.sum(log_keep, axis=-1, keepdims=True)
        return carry, acc

    carry = [jnp.zeros((tq, 1), F32) for _ in range(heads)]
    acc = [jnp.zeros((tq, SB_HEAD_DIM), F32) for _ in range(heads)]
    for jo in reversed(range(inner)):
        lo_row = jo * tk
        start = pl.multiple_of(i * tq + lo_row, tk)
        for h in range(heads):
            c_new, a_new = block(h, slice(lo_row, tq), start, carry[h][lo_row:tq],
                                 acc[h][lo_row:tq], True)
            carry[h] = jnp.concatenate([carry[h][0:lo_row], c_new], axis=0) if lo_row else c_new
            acc[h] = jnp.concatenate([acc[h][0:lo_row], a_new], axis=0) if lo_row else a_new

    def cond(state):
        j, carries, _ = state
        worst = functools.reduce(jnp.maximum, carries)
        return jnp.logical_and(j >= 0, jnp.max(worst) > EXP_UNDERFLOW)

    for lo_row in range(0, tq, tk):
        rows = slice(lo_row, lo_row + tk)

        def body(state, rows=rows):
            j, carries, accs = state
            start = pl.multiple_of(j * tk, tk)
            done = [block(h, rows, start, carries[h], accs[h], False) for h in range(heads)]
            return j - 1, tuple(c for c, _ in done), tuple(a for _, a in done)

        first = (i * inner - 1, tuple(c[rows] for c in carry), tuple(a[rows] for a in acc))
        _, _, parts = lax.while_loop(cond, body, first)
        for h in range(heads):
            gate = _silu(g_ref[0, rows, lanes[h]])
            o_ref[0, rows, lanes[h]] = (parts[h] * gate).astype(o_ref.dtype)


def _attention(qkv, gu, tq, tk, heads):
    b, s, _ = qkv.shape
    width = heads * SB_HEAD_DIM
    groups = SB_HEADS // heads
    kernel = functools.partial(_attn_kernel, tq=tq, tk=tk, heads=heads)
    return pl.pallas_call(
        kernel,
        grid=(b, groups, s // tq),
        in_specs=[pl.BlockSpec((1, tq, width), lambda bi, h, i: (bi, i, h)),
                  pl.BlockSpec((1, s, width), lambda bi, h, i: (bi, 0, groups + h)),
                  pl.BlockSpec((1, s, width), lambda bi, h, i: (bi, 0, 2 * groups + h)),
                  pl.BlockSpec((1, tq, width), lambda bi, h, i: (bi, i, h))],
        out_specs=pl.BlockSpec((1, tq, width), lambda bi, h, i: (bi, i, h)),
        out_shape=jax.ShapeDtypeStruct((b, s, SB_WIDTH), BF16),
        compiler_params=_params("parallel", "parallel", "arbitrary"),
        name="stickbreak_attn",
    )(qkv, qkv, qkv, gu)


def _even_tail_kernel(ma_ref, u_ref, halo_ref, gb_ref, x_ref, wp_ref, ps_ref, wo_ref,
                      lg_ref, lb_ref, o_ref, *, tm, seq):
    i = pl.program_id(0)
    first = (i * tm) % seq == 0
    u = u_ref[...]
    halo = jnp.where(first, 0.0, halo_ref[...])
    ext = jnp.concatenate([halo, u], axis=0)
    pos = (i * tm) % seq + lax.broadcasted_iota(jnp.int32, (tm, 1), 0)

    y = jnp.dot(ma_ref[...], wo_ref[0:SB_WIDTH, :], preferred_element_type=F32)
    for g, window in enumerate(POOL_WINDOWS):
        lo = g * POOL_GROUP_DIM
        hi = lo + POOL_GROUP_DIM
        acc = ext[:, lo:hi]
        span = 1
        while span < window:
            acc = acc + pltpu.roll(acc, span, 0)
            span *= 2
        count = jnp.minimum(pos + 1, window).astype(F32)
        pooled = acc[POOL_HALO:, :] / count - u[:, lo:hi]
        mixed = _dot(pooled, wp_ref[g]) * ps_ref[:, lo:hi]
        mb = mixed * _silu(gb_ref[:, lo:hi])
        y = y + _dot(mb, wo_ref[SB_WIDTH + lo:SB_WIDTH + hi, :])

    z = DEEPNORM_ALPHA * x_ref[...] + y
    o_ref[...] = _layer_norm_rows(z, lg_ref[...], lb_ref[...])


def _even_tail(ma, gu, x, w_pool, pool_scale, w_out, ln_g, ln_b, seq, tm):
    m = x.shape[0]
    halo_blocks = tm // POOL_HALO
    u_col = SB_WIDTH // POOL_WIDTH
    kernel = functools.partial(_even_tail_kernel, tm=tm, seq=seq)
    return pl.pallas_call(
        kernel,
        grid=(m // tm,),
        in_specs=[pl.BlockSpec((tm, SB_WIDTH), lambda i: (i, 0)),
                  pl.BlockSpec((tm, POOL_WIDTH), lambda i: (i, u_col)),
                  pl.BlockSpec((POOL_HALO, POOL_WIDTH),
                               lambda i: (jnp.maximum(i * halo_blocks - 1, 0), u_col)),
                  pl.BlockSpec((tm, POOL_WIDTH), lambda i: (i, u_col + 1)),
                  pl.BlockSpec((tm, D_MODEL), lambda i: (i, 0)),
                  pl.BlockSpec(w_pool.shape, lambda i: (0, 0, 0)),
                  pl.BlockSpec((1, POOL_WIDTH), lambda i: (0, 0)),
                  pl.BlockSpec(w_out.shape, lambda i: (0, 0)),
                  pl.BlockSpec((1, D_MODEL), lambda i: (0, 0)),
                  pl.BlockSpec((1, D_MODEL), lambda i: (0, 0))],
        out_specs=pl.BlockSpec((tm, D_MODEL), lambda i: (i, 0)),
        out_shape=jax.ShapeDtypeStruct((m, D_MODEL), F32),
        compiler_params=_params("parallel"),
        name="even_tail",
    )(ma, gu, gu, gu, x, w_pool, pool_scale, w_out, ln_g, ln_b)


def _shifted_delta(x, halo_ref, first):
    last = jnp.where(first, 0.0, halo_ref[SUBLANES - 1:SUBLANES, :])
    row = lax.broadcasted_iota(jnp.int32, (x.shape[0], 1), 0)
    prev = jnp.where(row == 0, last, pltpu.roll(x, 1, 0))
    return prev - x


def _shift_proj_kernel(x_ref, halo_ref, mu_ref, w_ref, o_ref, wb_ref, *, tm, seq):
    i = pl.program_id(0)

    @pl.when(i == 0)
    def _():
        wb_ref[...] = w_ref[...].astype(BF16)

    x = x_ref[...]
    xx = _shifted_delta(x, halo_ref, (i * tm) % seq == 0)
    o_ref[...] = _dot(x + xx * mu_ref[...], wb_ref[...])


def _shift_project(x, mu, w, seq, tm):
    m = x.shape[0]
    halo_blocks = tm // SUBLANES
    kernel = functools.partial(_shift_proj_kernel, tm=tm, seq=seq)
    return pl.pallas_call(
        kernel,
        grid=(m // tm,),
        in_specs=[pl.BlockSpec((tm, D_MODEL), lambda i: (i, 0)),
                  pl.BlockSpec((SUBLANES, D_MODEL),
                               lambda i: (jnp.maximum(i * halo_blocks - 1, 0), 0)),
                  pl.BlockSpec((1, D_MODEL), lambda i: (0, 0)),
                  pl.BlockSpec((D_MODEL, D_MODEL), lambda i: (0, 0), pipeline_mode=pl.Buffered(1))],
        out_specs=pl.BlockSpec((tm, D_MODEL), lambda i: (i, 0)),
        out_shape=jax.ShapeDtypeStruct((m, D_MODEL), F32),
        scratch_shapes=[pltpu.VMEM((D_MODEL, D_MODEL), BF16)],
        compiler_params=_params("arbitrary"),
        name="shift_proj",
    )(x, x, mu, w)


def _lora_kernel(x_ref, halo_ref, mu_ref, w1_ref, w2_ref, w0_ref, a1_ref, a2_ref, a0_ref,
                 lw_ref, a_ref, *, tm, seq):
    i = pl.program_id(0)
    x = x_ref[...]
    xx = _shifted_delta(x, halo_ref, (i * tm) % seq == 0)
    xw = x + xx * mu_ref[0:1, :]
    xa = x + xx * mu_ref[1:2, :]
    ww = w0_ref[...] + _dot(jnp.tanh(_dot(xw, w1_ref[...])), w2_ref[...])
    lw_ref[...] = -DECAY_SCALE / (1.0 + jnp.exp(-ww))
    aa = a0_ref[...] + _dot(_dot(xa, a1_ref[...]), a2_ref[...])
    a_ref[...] = 1.0 / (1.0 + jnp.exp(-aa))


def _lora(x, mu2, w1, w2, w0, a1, a2, a0, seq, tm):
    m = x.shape[0]
    halo_blocks = tm // SUBLANES
    kernel = functools.partial(_lora_kernel, tm=tm, seq=seq)
    full = lambda arr: pl.BlockSpec(arr.shape, lambda i: (0,) * arr.ndim)
    out = jax.ShapeDtypeStruct((m, D_MODEL), F32)
    return pl.pallas_call(
        kernel,
        grid=(m // tm,),
        in_specs=[pl.BlockSpec((tm, D_MODEL), lambda i: (i, 0)),
                  pl.BlockSpec((SUBLANES, D_MODEL),
                               lambda i: (jnp.maximum(i * halo_blocks - 1, 0), 0)),
                  full(mu2), full(w1), full(w2), full(w0), full(a1), full(a2), full(a0)],
        out_specs=[pl.BlockSpec((tm, D_MODEL), lambda i: (i, 0)),
                   pl.BlockSpec((tm, D_MODEL), lambda i: (i, 0))],
        out_shape=[out, out],
        compiler_params=_params("parallel"),
        name="decay_iclr_lora",
    )(x, x, mu2, w1, w2, w0, a1, a2, a0)


def _blockdiag(y, half_masks):
    yb = y.astype(BF16)
    zero = jnp.zeros((y.shape[0], LANES), BF16)
    tiles = y.shape[1] // LANES
    rows = []
    for h in range(WKV_HEADS):
        mine = h // 2
        kept = yb[:, mine * LANES:(mine + 1) * LANES] * half_masks[h % 2]
        rows.append(jnp.concatenate([kept if tl == mine else zero for tl in range(tiles)], axis=1))
    return jnp.concatenate(rows, axis=0)


def _fold_diag_blocks(full, head_masks):
    c = WKV_CHUNK
    out = jnp.where(head_masks[0], full[0:c, :], 0.0)
    for h in range(1, WKV_HEADS):
        out = out + jnp.where(head_masks[h], full[h * c:(h + 1) * c, :], 0.0)
    return out


def _wkv_kernel(r_ref, k_ref, v_ref, g_ref, lw_ref, a_ref, kk_ref, ka_ref, rk_ref,
                gw_ref, gb_ref, o_ref, state_ref, *, chunks, quads):
    @pl.when(pl.program_id(2) == 0)
    def _():
        state_ref[...] = jnp.zeros_like(state_ref)

    c = WKV_CHUNK
    w = WKV_LANES
    t = chunks * c
    srow = lax.broadcasted_iota(jnp.int32, (w, w), 0)
    scol = lax.broadcasted_iota(jnp.int32, (w, w), 1)
    bd_mask = jnp.where((srow // c) == (scol // c), 1.0, 0.0).astype(BF16)
    lane = lax.broadcasted_iota(jnp.int32, (1, w), 1)
    head_masks = [(lane // c) == h for h in range(WKV_HEADS)]
    crow = lax.broadcasted_iota(jnp.int32, (c, w), 0)
    ccol = lax.broadcasted_iota(jnp.int32, (c, w), 1) % c
    strict = crow > ccol
    lower = crow >= ccol
    eye_cat = jnp.where(crow == ccol, 1.0, 0.0)
    tg = min(t, w)
    trow = lax.broadcasted_iota(jnp.int32, (tg, tg), 0)
    tcol = lax.broadcasted_iota(jnp.int32, (tg, tg), 1)
    prefix = jnp.where(jnp.logical_and((trow // c) == (tcol // c), trow >= tcol),
                       1.0, 0.0).astype(BF16)

    def prepare(qi):
        lanes = slice(qi * w, (qi + 1) * w)
        r = r_ref[0, :, lanes]
        k = k_ref[0, :, lanes]
        v = v_ref[0, :, lanes]
        lw = lw_ref[0, :, lanes]
        a = a_ref[0, :, lanes]

        kk = k * kk_ref[:, lanes]
        kmod = k * (1.0 + (a - 1.0) * ka_ref[:, lanes])
        sums = _dot(jnp.concatenate([kk * kk, r * kmod * rk_ref[:, lanes]], axis=0), bd_mask)
        kk = kk / jnp.maximum(jnp.sqrt(sums[0:t]), L2_EPS)
        b = a * kk

        lw_hi = lw.astype(BF16)
        lw_lo = (lw - lw_hi.astype(F32)).astype(BF16)
        lw_split = jnp.concatenate([lw_hi, lw_lo], axis=1)
        cum2 = jnp.concatenate([jnp.dot(prefix, lw_split[s:s + tg], preferred_element_type=F32)
                                for s in range(0, t, tg)], axis=0)
        cum = cum2[:, 0:w] + cum2[:, w:2 * w]
        totals = [cum[(ci + 1) * c - 1:(ci + 1) * c, :] for ci in range(chunks)]
        total_rows = jnp.concatenate([jnp.broadcast_to(tot, (c, w)) for tot in totals], axis=0)
        g_inv = jnp.exp(-cum)
        g_tail = jnp.exp(total_rows - cum)
        whole = dict(kkg=kk * jnp.exp(cum - lw), rg=r * jnp.exp(cum), kd=kmod * g_inv,
                     bd=b * g_inv, kt=kmod * g_tail, bt=b * g_tail, v=v)
        per_chunk = [{name: x[ci * c:(ci + 1) * c] for name, x in whole.items()}
                     for ci in range(chunks)]
        for ci in range(chunks):
            per_chunk[ci]["total"] = totals[ci]
        return per_chunk, sums[t:2 * t] * v

    prepared = [prepare(qi) for qi in range(quads)]
    units = [(qi, ci) for ci in range(chunks) for qi in range(quads)]
    every = range(len(units))
    un = [prepared[qi][0][ci] for qi, ci in units]
    tile_lane = lax.broadcasted_iota(jnp.int32, (1, LANES), 1)
    half_masks = [jnp.where(tile_lane < RWKV_HEAD_DIM, 1.0, 0.0).astype(BF16),
                  jnp.where(tile_lane < RWKV_HEAD_DIM, 0.0, 1.0).astype(BF16)]
    bdm = lambda y: _blockdiag(y, half_masks)

    lhs = [jnp.concatenate([un[ci]["kkg"], un[ci]["rg"]], axis=0) for ci in every]
    a1 = [_dot_nt(lhs[ci], bdm(un[ci]["bd"])) for ci in every]
    a2 = [_dot_nt(lhs[ci], bdm(un[ci]["kd"])) for ci in every]
    a_kb = [jnp.where(strict, a1[ci][0:c], 0.0) for ci in every]
    a_rb = [jnp.where(lower, a1[ci][c:2 * c], 0.0) for ci in every]
    a_kk = [jnp.where(strict, a2[ci][0:c], 0.0) for ci in every]
    a_rk = [jnp.where(lower, a2[ci][c:2 * c], 0.0) for ci in every]

    power = [-a_kb[ci] for ci in every]
    inv = [eye_cat + power[ci] for ci in every]
    power = [_dot(power[ci], bdm(power[ci])) for ci in every]
    span = 2
    while 2 * span < c:
        both = [_dot(jnp.concatenate([inv[ci], power[ci]], axis=0), bdm(power[ci])) for ci in every]
        inv = [inv[ci] + both[ci][0:c] for ci in every]
        power = [both[ci][c:2 * c] for ci in every]
        span *= 2
    inv = [inv[ci] + _dot(inv[ci], bdm(power[ci])) for ci in every]

    av = [_dot(jnp.concatenate([a_kk[ci], a_rk[ci]], axis=0), bdm(un[ci]["v"])) for ci in every]
    akkv = [av[ci][0:c] for ci in every]
    arkv = [av[ci][c:2 * c] for ci in every]
    wu = [_dot(inv[ci], jnp.concatenate([bdm(un[ci]["kkg"]), bdm(akkv[ci])], axis=1)) for ci in every]
    w1 = [wu[ci][:, 0:w] for ci in every]
    u0 = [wu[ci][:, w:2 * w] for ci in every]
    arb = [_dot(a_rb[ci], jnp.concatenate([bdm(w1[ci]), bdm(u0[ci])], axis=1)) for ci in every]
    r2 = [un[ci]["rg"] - arb[ci][:, 0:w] for ci in every]
    o0 = [arkv[ci] - arb[ci][:, w:2 * w] for ci in every]
    trans = [jnp.where(crow == ccol, jnp.exp(un[ci]["total"]), 0.0)
             - _fold_diag_blocks(_dot_tn(un[ci]["bt"], w1[ci]), head_masks) for ci in every]
    fresh = [_fold_diag_blocks(
        _dot_tn(jnp.concatenate([un[ci]["kt"], -un[ci]["bt"]], axis=0),
                jnp.concatenate([un[ci]["v"], u0[ci]], axis=0)), head_masks) for ci in every]

    states = [state_ref[qi] for qi in range(quads)]
    outs = [[] for _ in range(quads)]
    for ui, (qi, _) in enumerate(units):
        both = _dot(jnp.concatenate([trans[ui], r2[ui]], axis=0), bdm(states[qi]))
        states[qi] = both[0:c] + fresh[ui]
        outs[qi].append(both[c:2 * c] + o0[ui])

    inv_n = 1.0 / RWKV_HEAD_DIM
    for qi in range(quads):
        lanes = slice(qi * w, (qi + 1) * w)
        state_ref[qi] = states[qi]
        o = jnp.concatenate(outs[qi], axis=0)
        d = o - _dot(o, bd_mask) * inv_n
        var = _dot(d * d, bd_mask) * inv_n
        normed = d * lax.rsqrt(var + GN_EPS) * gw_ref[:, lanes] + gb_ref[:, lanes]
        gate = _silu(g_ref[0, :, lanes])
        o_ref[0, :, lanes] = ((normed + prepared[qi][1]) * gate).astype(o_ref.dtype)


def _wkv(r, k, v, g, lw, a, k_k, k_a, r_k, gn_w, gn_b, chunks, quads):
    b, s, d = r.shape
    t = chunks * WKV_CHUNK
    width = quads * WKV_LANES
    kernel = functools.partial(_wkv_kernel, chunks=chunks, quads=quads)
    tok = pl.BlockSpec((1, t, width), lambda bi, h, ci: (bi, ci, h))
    vec = pl.BlockSpec((1, width), lambda bi, h, ci: (0, h))
    return pl.pallas_call(
        kernel,
        grid=(b, d // width, s // t),
        in_specs=[tok, tok, tok, tok, tok, tok, vec, vec, vec, vec, vec],
        out_specs=tok,
        out_shape=jax.ShapeDtypeStruct((b, s, d), BF16),
        scratch_shapes=[pltpu.VMEM((quads, RWKV_HEAD_DIM, WKV_LANES), F32)],
        compiler_params=_params("parallel", "parallel", "arbitrary"),
        name="wkv7_chunked",
    )(r, k, v, g, lw, a, k_k, k_a, r_k, gn_w, gn_b)


def _odd_tail_kernel(m_ref, x_ref, wo_ref, lg_ref, lb_ref, o_ref, *, parts):
    rows_per = m_ref.shape[0] // parts
    for p in range(parts):
        rows = slice(p * rows_per, (p + 1) * rows_per)
        y = jnp.dot(m_ref[rows, :], wo_ref[...], preferred_element_type=F32)
        z = DEEPNORM_ALPHA * x_ref[rows, :] + y
        o_ref[rows, :] = _layer_norm_rows(z, lg_ref[...], lb_ref[...])


def _odd_tail(mixed, x, w_o, ln_g, ln_b, tm):
    m = x.shape[0]
    return pl.pallas_call(
        functools.partial(_odd_tail_kernel, parts=4),
        grid=(m // tm,),
        in_specs=[pl.BlockSpec((tm, D_MODEL), lambda i: (i, 0)),
                  pl.BlockSpec((tm, D_MODEL), lambda i: (i, 0)),
                  pl.BlockSpec(w_o.shape, lambda i: (0, 0)),
                  pl.BlockSpec((1, D_MODEL), lambda i: (0, 0)),
                  pl.BlockSpec((1, D_MODEL), lambda i: (0, 0))],
        out_specs=pl.BlockSpec((tm, D_MODEL), lambda i: (i, 0)),
        out_shape=jax.ShapeDtypeStruct((m, D_MODEL), F32),
        compiler_params=_params("parallel"),
        name="odd_tail",
    )(mixed, x, w_o, ln_g, ln_b)


def _even_layer(x2, seq, w_in, w_pool, pool_scale, w_out, ln_g, ln_b):
    m = x2.shape[0]
    batch = m // seq
    split = 3 * SB_WIDTH
    w_in = w_in.astype(BF16)
    q_scale = jnp.concatenate([jnp.full((1, SB_WIDTH), SB_HEAD_DIM ** -0.5, F32),
                               jnp.ones((1, split - SB_WIDTH), F32)], axis=1)
    qkv = _project(x2, w_in, 0, q_scale, BF16, tm=1024, tn=1024)
    gu = _project(x2, w_in, split, jnp.ones((1, split), F32), F32, tm=1024, tn=1024)
    ma = _attention(qkv.reshape(batch, seq, split), gu.reshape(batch, seq, split),
                    tq=512, tk=256, heads=8)
    return _even_tail(ma.reshape(m, SB_WIDTH), gu, x2, w_pool.astype(BF16),
                      pool_scale.reshape(1, POOL_WIDTH), w_out.astype(BF16),
                      ln_g.reshape(1, D_MODEL), ln_b.reshape(1, D_MODEL), seq, tm=512)


def _pad_rank(w, axis):
    pad = [(0, 0), (0, 0)]
    pad[axis] = (0, LORA_PAD - w.shape[axis])
    return jnp.pad(w, pad).astype(BF16)


def _odd_layer(x2, seq, mu, w_r, w_k, w_v, w_g, w0, w1, w2, a0, a1, a2, k_k, k_a, r_k,
               gn_w, gn_b, w_o, ln_g, ln_b):
    m = x2.shape[0]
    batch = m // seq
    row = lambda vec: vec.reshape(1, D_MODEL)
    r, k, v, g = (_shift_project(x2, row(mu[j]), w, seq, tm=512)
                  for j, w in ((0, w_r), (2, w_k), (3, w_v), (5, w_g)))
    mu2 = jnp.stack([mu[1], mu[4]])
    lw, a = _lora(x2, mu2, _pad_rank(w1, 1), _pad_rank(w2, 0), row(w0),
                  _pad_rank(a1, 1), _pad_rank(a2, 0), row(a0), seq, tm=256)
    tokens = lambda t2: t2.reshape(batch, seq, D_MODEL)
    mixed = _wkv(tokens(r), tokens(k), tokens(v), tokens(g), tokens(lw), tokens(a),
                 row(k_k), row(k_a), row(r_k), row(gn_w), row(gn_b), chunks=4, quads=4)
    return _odd_tail(mixed.reshape(m, D_MODEL), x2, w_o.astype(BF16), row(ln_g), row(ln_b), tm=512)


def kernel(x, ev_w_in, ev_w_pool, ev_pool_scale, ev_w_out, od_mu, od_w_r, od_w_k, od_w_v, od_w_g,
           od_w0, od_w1, od_w2, od_a0, od_a1, od_a2, od_k_k, od_k_a, od_r_k, od_gn_w, od_gn_b,
           od_w_o, ln_g, ln_b):
    batch, seq, d = x.shape
    x2 = x.reshape(batch * seq, d)
    for layer in range(N_LAYERS):
        j = layer // 2
        if layer % 2 == 0:
            x2 = _even_layer(x2, seq, ev_w_in[j], ev_w_pool[j], ev_pool_scale[j], ev_w_out[j],
                             ln_g[layer], ln_b[layer])
        else:
            x2 = _odd_layer(x2, seq, od_mu[j], od_w_r[j], od_w_k[j], od_w_v[j], od_w_g[j],
                            od_w0[j], od_w1[j], od_w2[j], od_a0[j], od_a1[j], od_a2[j],
                            od_k_k[j], od_k_a[j], od_r_k[j], od_gn_w[j], od_gn_b[j], od_w_o[j],
                            ln_g[layer], ln_b[layer])
    return x2.reshape(batch, seq, d)
```

```python
import functools
import math

import jax
import jax.numpy as jnp
from jax import lax
from jax.experimental import pallas as pl
from jax.experimental.pallas import tpu as pltpu

F32 = jnp.float32
BF16 = jnp.bfloat16

D_MODEL = 2048
N_LAYERS = 2
SB_HEADS = 8
SB_HEAD_DIM = 128
SB_WIDTH = SB_HEADS * SB_HEAD_DIM
POOL_WINDOWS = (2, 4, 8, 16)
POOL_GROUP_DIM = 256
POOL_WIDTH = len(POOL_WINDOWS) * POOL_GROUP_DIM
POOL_HALO = 16
RWKV_HEAD_DIM = 64
LORA_PAD = 128
GN_EPS = 64e-5
L2_EPS = 1e-12
LN_EPS = 1e-5
DEEPNORM_ALPHA = (2 * N_LAYERS) ** 0.25

SUBLANES = 8
LANES = 128
VMEM_LIMIT = 52 * 1024 * 1024

EXP_UNDERFLOW = -104.0
DECAY_SCALE = math.exp(-0.5)

WKV_CHUNK = 64
WKV_HEADS = 4
WKV_LANES = WKV_HEADS * RWKV_HEAD_DIM
assert WKV_CHUNK == RWKV_HEAD_DIM


def _dot(a, b):
    return jnp.dot(a.astype(BF16), b.astype(BF16), preferred_element_type=F32)


def _dot_nt(a, b):
    return lax.dot_general(a.astype(BF16), b.astype(BF16), (((1,), (1,)), ((), ())),
                           preferred_element_type=F32)


def _dot_tn(a, b):
    return lax.dot_general(a.astype(BF16), b.astype(BF16), (((0,), (0,)), ((), ())),
                           preferred_element_type=F32)


def _silu(g):
    return g / (1.0 + jnp.exp(-g))


def _layer_norm_rows(z, g, b):
    mean = jnp.mean(z, axis=-1, keepdims=True)
    d = z - mean
    var = jnp.mean(d * d, axis=-1, keepdims=True)
    return d * lax.rsqrt(var + LN_EPS) * g + b


def _params(*sem):
    return pltpu.CompilerParams(dimension_semantics=sem, vmem_limit_bytes=VMEM_LIMIT)


def _proj_kernel(x_ref, w_ref, s_ref, o_ref):
    acc = jnp.dot(x_ref[...].astype(BF16), w_ref[...], preferred_element_type=F32)
    o_ref[...] = (acc * s_ref[...]).astype(o_ref.dtype)


def _project(x, w, first_col, col_scale, out_dtype, tm):
    m, k = x.shape
    n = col_scale.shape[1]
    skip = first_col // n
    return pl.pallas_call(
        _proj_kernel,
        grid=(m // tm,),
        in_specs=[pl.BlockSpec((tm, k), lambda i: (i, 0)),
                  pl.BlockSpec((k, n), lambda i: (0, skip), pipeline_mode=pl.Buffered(1)),
                  pl.BlockSpec((1, n), lambda i: (0, 0))],
        out_specs=pl.BlockSpec((tm, n), lambda i: (i, 0)),
        out_shape=jax.ShapeDtypeStruct((m, n), out_dtype),
        compiler_params=_params("parallel"),
        name="in_proj",
    )(x, w, col_scale)


def _attn_kernel(q_ref, k_ref, v_ref, g_ref, o_ref, *, tq, tk, heads):
    i = pl.program_id(2)
    inner = tq // tk
    krow = lax.broadcasted_iota(jnp.int32, (tk, tk), 0)
    kcol = lax.broadcasted_iota(jnp.int32, (tk, tk), 1)
    after = jnp.where(krow > kcol, 1.0, 0.0).astype(BF16)
    lanes = [slice(h * SB_HEAD_DIM, (h + 1) * SB_HEAD_DIM) for h in range(heads)]

    def block(h, rows, start, carry, acc, diagonal):
        q = q_ref[0, rows, lanes[h]]
        kb = k_ref[0, pl.ds(start, tk), lanes[h]]
        vb = v_ref[0, pl.ds(start, tk), lanes[h]]
        z = _dot_nt(q, kb)
        log_beta = jnp.minimum(z, 0.0) - jnp.log(1.0 + jnp.exp(-jnp.abs(z)))
        log_keep = log_beta - z
        if diagonal:
            causal = (lax.broadcasted_iota(jnp.int32, z.shape, 1)
                      < lax.broadcasted_iota(jnp.int32, z.shape, 0))
            log_keep = jnp.where(causal, log_keep, 0.0)
        later = jnp.dot(log_keep.astype(BF16), after, preferred_element_type=F32) + carry
        w = jnp.exp(log_beta + later)
        if diagonal:
            w = jnp.where(causal, w, 0.0)
        acc = acc + _dot(w, vb)
        carry = carry + jnp.sum(log_keep, axis=-1, keepdims=True)
        return carry, acc

    carry = [jnp.zeros((tq, 1), F32) for _ in range(heads)]
    acc = [jnp.zeros((tq, SB_HEAD_DIM), F32) for _ in range(heads)]
    for jo in reversed(range(inner)):
        lo_row = jo * tk
        start = pl.multiple_of(i * tq + lo_row, tk)
        for h in range(heads):
            c_new, a_new = block(h, slice(lo_row, tq), start, carry[h][lo_row:tq],
                                 acc[h][lo_row:tq], True)
            carry[h] = jnp.concatenate([carry[h][0:lo_row], c_new], axis=0) if lo_row else c_new
            acc[h] = jnp.concatenate([acc[h][0:lo_row], a_new], axis=0) if lo_row else a_new

    def cond(state):
        j, carries, _ = state
        worst = functools.reduce(jnp.maximum, carries)
        return jnp.logical_and(j >= 0, jnp.max(worst) > EXP_UNDERFLOW)

    for lo_row in range(0, tq, tk):
        rows = slice(lo_row, lo_row + tk)

        def body(state, rows=rows):
            j, carries, accs = state
            start = pl.multiple_of(j * tk, tk)
            done = [block(h, rows, start, carries[h], accs[h], False) for h in range(heads)]
            return j - 1, tuple(c for c, _ in done), tuple(a for _, a in done)

        first = (i * inner - 1, tuple(c[rows] for c in carry), tuple(a[rows] for a in acc))
        _, _, parts = lax.while_loop(cond, body, first)
        for h in range(heads):
            gate = _silu(g_ref[0, rows, lanes[h]])
            o_ref[0, rows, lanes[h]] = (parts[h] * gate).astype(o_ref.dtype)


def _attention(qkv, gu, tq, tk, heads):
    b, s, _ = qkv.shape
    width = heads * SB_HEAD_DIM
    groups = SB_HEADS // heads
    kernel = functools.partial(_attn_kernel, tq=tq, tk=tk, heads=heads)
    return pl.pallas_call(
        kernel,
        grid=(b, groups, s // tq),
        in_specs=[pl.BlockSpec((1, tq, width), lambda bi, h, i: (bi, i, h)),
                  pl.BlockSpec((1, s, width), lambda bi, h, i: (bi, 0, groups + h)),
                  pl.BlockSpec((1, s, width), lambda bi, h, i: (bi, 0, 2 * groups + h)),
                  pl.BlockSpec((1, tq, width), lambda bi, h, i: (bi, i, h))],
        out_specs=pl.BlockSpec((1, tq, width), lambda bi, h, i: (bi, i, h)),
        out_shape=jax.ShapeDtypeStruct((b, s, SB_WIDTH), BF16),
        compiler_params=_params("parallel", "parallel", "arbitrary"),
        name="stickbreak_attn",
    )(qkv, qkv, qkv, gu)


def _even_tail_kernel(ma_ref, u_ref, halo_ref, gb_ref, x_ref, wp_ref, ps_ref, wo_ref,
                      lg_ref, lb_ref, o_ref, *, tm, seq):
    i = pl.program_id(0)
    first = (i * tm) % seq == 0
    u = u_ref[...]
    halo = jnp.where(first, 0.0, halo_ref[...])
    ext = jnp.concatenate([halo, u], axis=0)
    pos = (i * tm) % seq + lax.broadcasted_iota(jnp.int32, (tm, 1), 0)

    y = jnp.dot(ma_ref[...], wo_ref[0:SB_WIDTH, :], preferred_element_type=F32)
    for g, window in enumerate(POOL_WINDOWS):
        lo = g * POOL_GROUP_DIM
        hi = lo + POOL_GROUP_DIM
        acc = ext[:, lo:hi]
        span = 1
        while span < window:
            acc = acc + pltpu.roll(acc, span, 0)
            span *= 2
        count = jnp.minimum(pos + 1, window).astype(F32)
        pooled = acc[POOL_HALO:, :] / count - u[:, lo:hi]
        mixed = _dot(pooled, wp_ref[g]) * ps_ref[:, lo:hi]
        mb = mixed * _silu(gb_ref[:, lo:hi])
        y = y + _dot(mb, wo_ref[SB_WIDTH + lo:SB_WIDTH + hi, :])

    z = DEEPNORM_ALPHA * x_ref[...] + y
    o_ref[...] = _layer_norm_rows(z, lg_ref[...], lb_ref[...])


def _even_tail(ma, gu, x, w_pool, pool_scale, w_out, ln_g, ln_b, seq, tm):
    m = x.shape[0]
    halo_blocks = tm // POOL_HALO
    u_col = SB_WIDTH // POOL_WIDTH
    kernel = functools.partial(_even_tail_kernel, tm=tm, seq=seq)
    return pl.pallas_call(
        kernel,
        grid=(m // tm,),
        in_specs=[pl.BlockSpec((tm, SB_WIDTH), lambda i: (i, 0)),
                  pl.BlockSpec((tm, POOL_WIDTH), lambda i: (i, u_col)),
                  pl.BlockSpec((POOL_HALO, POOL_WIDTH),
                               lambda i: (jnp.maximum(i * halo_blocks - 1, 0), u_col)),
                  pl.BlockSpec((tm, POOL_WIDTH), lambda i: (i, u_col + 1)),
                  pl.BlockSpec((tm, D_MODEL), lambda i: (i, 0)),
                  pl.BlockSpec(w_pool.shape, lambda i: (0, 0, 0)),
                  pl.BlockSpec((1, POOL_WIDTH), lambda i: (0, 0)),
                  pl.BlockSpec(w_out.shape, lambda i: (0, 0)),
                  pl.BlockSpec((1, D_MODEL), lambda i: (0, 0)),
                  pl.BlockSpec((1, D_MODEL), lambda i: (0, 0))],
        out_specs=pl.BlockSpec((tm, D_MODEL), lambda i: (i, 0)),
        out_shape=jax.ShapeDtypeStruct((m, D_MODEL), F32),
        compiler_params=_params("parallel"),
        name="even_tail",
    )(ma, gu, gu, gu, x, w_pool, pool_scale, w_out, ln_g, ln_b)


def _shifted_delta(x, halo_ref, first):
    last = jnp.where(first, 0.0, halo_ref[SUBLANES - 1:SUBLANES, :])
    row = lax.broadcasted_iota(jnp.int32, (x.shape[0], 1), 0)
    prev = jnp.where(row == 0, last, pltpu.roll(x, 1, 0))
    return prev - x


def _shift_proj_kernel(x_ref, halo_ref, mu_ref, w_ref, o_ref, wb_ref, *, tm, seq):
    i = pl.program_id(0)

    @pl.when(i == 0)
    def _():
        wb_ref[...] = w_ref[...].astype(BF16)

    x = x_ref[...]
    xx = _shifted_delta(x, halo_ref, (i * tm) % seq == 0)
    o_ref[...] = _dot(x + xx * mu_ref[...], wb_ref[...])


def _shift_project(x, mu, w, seq, tm):
    m = x.shape[0]
    halo_blocks = tm // SUBLANES
    kernel = functools.partial(_shift_proj_kernel, tm=tm, seq=seq)
    return pl.pallas_call(
        kernel,
        grid=(m // tm,),
        in_specs=[pl.BlockSpec((tm, D_MODEL), lambda i: (i, 0)),
                  pl.BlockSpec((SUBLANES, D_MODEL),
                               lambda i: (jnp.maximum(i * halo_blocks - 1, 0), 0)),
                  pl.BlockSpec((1, D_MODEL), lambda i: (0, 0)),
                  pl.BlockSpec((D_MODEL, D_MODEL), lambda i: (0, 0), pipeline_mode=pl.Buffered(1))],
        out_specs=pl.BlockSpec((tm, D_MODEL), lambda i: (i, 0)),
        out_shape=jax.ShapeDtypeStruct((m, D_MODEL), F32),
        scratch_shapes=[pltpu.VMEM((D_MODEL, D_MODEL), BF16)],
        compiler_params=_params("arbitrary"),
        name="shift_proj",
    )(x, x, mu, w)


def _lora_kernel(x_ref, halo_ref, mu_ref, w1_ref, w2_ref, w0_ref, a1_ref, a2_ref, a0_ref,
                 lw_ref, a_ref, *, tm, seq):
    i = pl.program_id(0)
    x = x_ref[...]
    xx = _shifted_delta(x, halo_ref, (i * tm) % seq == 0)
    xw = x + xx * mu_ref[0:1, :]
    xa = x + xx * mu_ref[1:2, :]
    ww = w0_ref[...] + _dot(jnp.tanh(_dot(xw, w1_ref[...])), w2_ref[...])
    lw_ref[...] = -DECAY_SCALE / (1.0 + jnp.exp(-ww))
    aa = a0_ref[...] + _dot(_dot(xa, a1_ref[...]), a2_ref[...])
    a_ref[...] = 1.0 / (1.0 + jnp.exp(-aa))


def _lora(x, mu2, w1, w2, w0, a1, a2, a0, seq, tm):
    m = x.shape[0]
    halo_blocks = tm // SUBLANES
    kernel = functools.partial(_lora_kernel, tm=tm, seq=seq)
    full = lambda arr: pl.BlockSpec(arr.shape, lambda i: (0,) * arr.ndim)
    out = jax.ShapeDtypeStruct((m, D_MODEL), F32)
    return pl.pallas_call(
        kernel,
        grid=(m // tm,),
        in_specs=[pl.BlockSpec((tm, D_MODEL), lambda i: (i, 0)),
                  pl.BlockSpec((SUBLANES, D_MODEL),
                               lambda i: (jnp.maximum(i * halo_blocks - 1, 0), 0)),
                  full(mu2), full(w1), full(w2), full(w0), full(a1), full(a2), full(a0)],
        out_specs=[pl.BlockSpec((tm, D_MODEL), lambda i: (i, 0)),
                   pl.BlockSpec((tm, D_MODEL), lambda i: (i, 0))],
        out_shape=[out, out],
        compiler_params=_params("parallel"),
        name="decay_iclr_lora",
    )(x, x, mu2, w1, w2, w0, a1, a2, a0)


def _blockdiag(y, half_masks):
    yb = y.astype(BF16)
    zero = jnp.zeros((y.shape[0], LANES), BF16)
    tiles = y.shape[1] // LANES
    rows = []
    for h in range(WKV_HEADS):
        mine = h // 2
        kept = yb[:, mine * LANES:(mine + 1) * LANES] * half_masks[h % 2]
        rows.append(jnp.concatenate([kept if tl == mine else zero for tl in range(tiles)], axis=1))
    return jnp.concatenate(rows, axis=0)


def _fold_diag_blocks(full, head_masks):
    c = WKV_CHUNK
    out = jnp.where(head_masks[0], full[0:c, :], 0.0)
    for h in range(1, WKV_HEADS):
        out = out + jnp.where(head_masks[h], full[h * c:(h + 1) * c, :], 0.0)
    return out


def _wkv_kernel(r_ref, k_ref, v_ref, g_ref, lw_ref, a_ref, kk_ref, ka_ref, rk_ref,
                gw_ref, gb_ref, o_ref, state_ref, *, chunks, quads):
    @pl.when(pl.program_id(2) == 0)
    def _():
        state_ref[...] = jnp.zeros_like(state_ref)

    c = WKV_CHUNK
    w = WKV_LANES
    t = chunks * c
    srow = lax.broadcasted_iota(jnp.int32, (w, w), 0)
    scol = lax.broadcasted_iota(jnp.int32, (w, w), 1)
    bd_mask = jnp.where((srow // c) == (scol // c), 1.0, 0.0).astype(BF16)
    lane = lax.broadcasted_iota(jnp.int32, (1, w), 1)
    head_masks = [(lane // c) == h for h in range(WKV_HEADS)]
    crow = lax.broadcasted_iota(jnp.int32, (c, w), 0)
    ccol = lax.broadcasted_iota(jnp.int32, (c, w), 1) % c
    strict = crow > ccol
    lower = crow >= ccol
    eye_cat = jnp.where(crow == ccol, 1.0, 0.0)
    tg = min(t, w)
    trow = lax.broadcasted_iota(jnp.int32, (tg, tg), 0)
    tcol = lax.broadcasted_iota(jnp.int32, (tg, tg), 1)
    prefix = jnp.where(jnp.logical_and((trow // c) == (tcol // c), trow >= tcol),
                       1.0, 0.0).astype(BF16)

    def prepare(qi):
        lanes = slice(qi * w, (qi + 1) * w)
        r = r_ref[0, :, lanes]
        k = k_ref[0, :, lanes]
        v = v_ref[0, :, lanes]
        lw = lw_ref[0, :, lanes]
        a = a_ref[0, :, lanes]

        kk = k * kk_ref[:, lanes]
        kmod = k * (1.0 + (a - 1.0) * ka_ref[:, lanes])
        sums = _dot(jnp.concatenate([kk * kk, r * kmod * rk_ref[:, lanes]], axis=0), bd_mask)
        kk = kk / jnp.maximum(jnp.sqrt(sums[0:t]), L2_EPS)
        b = a * kk

        lw_hi = lw.astype(BF16)
        lw_lo = (lw - lw_hi.astype(F32)).astype(BF16)
        lw_split = jnp.concatenate([lw_hi, lw_lo], axis=1)
        cum2 = jnp.concatenate([jnp.dot(prefix, lw_split[s:s + tg], preferred_element_type=F32)
                                for s in range(0, t, tg)], axis=0)
        cum = cum2[:, 0:w] + cum2[:, w:2 * w]
        totals = [cum[(ci + 1) * c - 1:(ci + 1) * c, :] for ci in range(chunks)]
        total_rows = jnp.concatenate([jnp.broadcast_to(tot, (c, w)) for tot in totals], axis=0)
        g_inv = jnp.exp(-cum)
        g_tail = jnp.exp(total_rows - cum)
        whole = dict(kkg=kk * jnp.exp(cum - lw), rg=r * jnp.exp(cum), kd=kmod * g_inv,
                     bd=b * g_inv, kt=kmod * g_tail, bt=b * g_tail, v=v)
        per_chunk = [{name: x[ci * c:(ci + 1) * c] for name, x in whole.items()}
                     for ci in range(chunks)]
        for ci in range(chunks):
            per_chunk[ci]["total"] = totals[ci]
        return per_chunk, sums[t:2 * t] * v

    prepared = [prepare(qi) for qi in range(quads)]
    units = [(qi, ci) for ci in range(chunks) for qi in range(quads)]
    every = range(len(units))
    un = [prepared[qi][0][ci] for qi, ci in units]
    tile_lane = lax.broadcasted_iota(jnp.int32, (1, LANES), 1)
    half_masks = [jnp.where(tile_lane < RWKV_HEAD_DIM, 1.0, 0.0).astype(BF16),
                  jnp.where(tile_lane < RWKV_HEAD_DIM, 0.0, 1.0).astype(BF16)]
    bdm = lambda y: _blockdiag(y, half_masks)

    lhs = [jnp.concatenate([un[ci]["kkg"], un[ci]["rg"]], axis=0) for ci in every]
    a1 = [_dot_nt(lhs[ci], bdm(un[ci]["bd"])) for ci in every]
    a2 = [_dot_nt(lhs[ci], bdm(un[ci]["kd"])) for ci in every]
    a_kb = [jnp.where(strict, a1[ci][0:c], 0.0) for ci in every]
    a_rb = [jnp.where(lower, a1[ci][c:2 * c], 0.0) for ci in every]
    a_kk = [jnp.where(strict, a2[ci][0:c], 0.0) for ci in every]
    a_rk = [jnp.where(lower, a2[ci][c:2 * c], 0.0) for ci in every]

    power = [-a_kb[ci] for ci in every]
    inv = [eye_cat + power[ci] for ci in every]
    power = [_dot(power[ci], bdm(power[ci])) for ci in every]
    span = 2
    while 2 * span < c:
        both = [_dot(jnp.concatenate([inv[ci], power[ci]], axis=0), bdm(power[ci])) for ci in every]
        inv = [inv[ci] + both[ci][0:c] for ci in every]
        power = [both[ci][c:2 * c] for ci in every]
        span *= 2
    inv = [inv[ci] + _dot(inv[ci], bdm(power[ci])) for ci in every]

    av = [_dot(jnp.concatenate([a_kk[ci], a_rk[ci]], axis=0), bdm(un[ci]["v"])) for ci in every]
    akkv = [av[ci][0:c] for ci in every]
    arkv = [av[ci][c:2 * c] for ci in every]
    wu = [_dot(inv[ci], jnp.concatenate([bdm(un[ci]["kkg"]), bdm(akkv[ci])], axis=1)) for ci in every]
    w1 = [wu[ci][:, 0:w] for ci in every]
    u0 = [wu[ci][:, w:2 * w] for ci in every]
    arb = [_dot(a_rb[ci], jnp.concatenate([bdm(w1[ci]), bdm(u0[ci])], axis=1)) for ci in every]
    r2 = [un[ci]["rg"] - arb[ci][:, 0:w] for ci in every]
    o0 = [arkv[ci] - arb[ci][:, w:2 * w] for ci in every]
    trans = [jnp.where(crow == ccol, jnp.exp(un[ci]["total"]), 0.0)
             - _fold_diag_blocks(_dot_tn(un[ci]["bt"], w1[ci]), head_masks) for ci in every]
    fresh = [_fold_diag_blocks(
        _dot_tn(jnp.concatenate([un[ci]["kt"], -un[ci]["bt"]], axis=0),
                jnp.concatenate([un[ci]["v"], u0[ci]], axis=0)), head_masks) for ci in every]

    states = [state_ref[qi] for qi in range(quads)]
    outs = [[] for _ in range(quads)]
    for ui, (qi, _) in enumerate(units):
        both = _dot(jnp.concatenate([trans[ui], r2[ui]], axis=0), bdm(states[qi]))
        states[qi] = both[0:c] + fresh[ui]
        outs[qi].append(both[c:2 * c] + o0[ui])

    inv_n = 1.0 / RWKV_HEAD_DIM
    for qi in range(quads):
        lanes = slice(qi * w, (qi + 1) * w)
        state_ref[qi] = states[qi]
        o = jnp.concatenate(outs[qi], axis=0)
        d = o - _dot(o, bd_mask) * inv_n
        var = _dot(d * d, bd_mask) * inv_n
        normed = d * lax.rsqrt(var + GN_EPS) * gw_ref[:, lanes] + gb_ref[:, lanes]
        gate = _silu(g_ref[0, :, lanes])
        o_ref[0, :, lanes] = ((normed + prepared[qi][1]) * gate).astype(o_ref.dtype)


def _wkv(r, k, v, g, lw, a, k_k, k_a, r_k, gn_w, gn_b, chunks, quads):
    b, s, d = r.shape
    t = chunks * WKV_CHUNK
    width = quads * WKV_LANES
    kernel = functools.partial(_wkv_kernel, chunks=chunks, quads=quads)
    tok = pl.BlockSpec((1, t, width), lambda bi, h, ci: (bi, ci, h))
    vec = pl.BlockSpec((1, width), lambda bi, h, ci: (0, h))
    return pl.pallas_call(
        kernel,
        grid=(b, d // width, s // t),
        in_specs=[tok, tok, tok, tok, tok, tok, vec, vec, vec, vec, vec],
        out_specs=tok,
        out_shape=jax.ShapeDtypeStruct((b, s, d), BF16),
        scratch_shapes=[pltpu.VMEM((quads, RWKV_HEAD_DIM, WKV_LANES), F32)],
        compiler_params=_params("parallel", "parallel", "arbitrary"),
        name="wkv7_chunked",
    )(r, k, v, g, lw, a, k_k, k_a, r_k, gn_w, gn_b)


def _odd_tail_kernel(m_ref, x_ref, wo_ref, lg_ref, lb_ref, o_ref, *, parts):
    rows_per = m_ref.shape[0] // parts
    for p in range(parts):
        rows = slice(p * rows_per, (p + 1) * rows_per)
        y = jnp.dot(m_ref[rows, :], wo_ref[...], preferred_element_type=F32)
        z = DEEPNORM_ALPHA * x_ref[rows, :] + y
        o_ref[rows, :] = _layer_norm_rows(z, lg_ref[...], lb_ref[...])


def _odd_tail(mixed, x, w_o, ln_g, ln_b, tm):
    m = x.shape[0]
    return pl.pallas_call(
        functools.partial(_odd_tail_kernel, parts=4),
        grid=(m // tm,),
        in_specs=[pl.BlockSpec((tm, D_MODEL), lambda i: (i, 0)),
                  pl.BlockSpec((tm, D_MODEL), lambda i: (i, 0)),
                  pl.BlockSpec(w_o.shape, lambda i: (0, 0)),
                  pl.BlockSpec((1, D_MODEL), lambda i: (0, 0)),
                  pl.BlockSpec((1, D_MODEL), lambda i: (0, 0))],
        out_specs=pl.BlockSpec((tm, D_MODEL), lambda i: (i, 0)),
        out_shape=jax.ShapeDtypeStruct((m, D_MODEL), F32),
        compiler_params=_params("parallel"),
        name="odd_tail",
    )(mixed, x, w_o, ln_g, ln_b)


def _even_layer(x2, seq, w_in, w_pool, pool_scale, w_out, ln_g, ln_b):
    m = x2.shape[0]
    batch = m // seq
    split = 3 * SB_WIDTH
    w_in = w_in.astype(BF16)
    q_scale = jnp.concatenate([jnp.full((1, SB_WIDTH), SB_HEAD_DIM ** -0.5, F32),
                               jnp.ones((1, split - SB_WIDTH), F32)], axis=1)
    qkv = _project(x2, w_in, 0, q_scale, BF16, tm=512)
    gu = _project(x2, w_in, split, jnp.ones((1, split), F32), F32, tm=512)
    ma = _attention(qkv.reshape(batch, seq, split), gu.reshape(batch, seq, split),
                    tq=512, tk=256, heads=8)
    return _even_tail(ma.reshape(m, SB_WIDTH), gu, x2, w_pool.astype(BF16),
                      pool_scale.reshape(1, POOL_WIDTH), w_out.astype(BF16),
                      ln_g.reshape(1, D_MODEL), ln_b.reshape(1, D_MODEL), seq, tm=512)


def _pad_rank(w, axis):
    pad = [(0, 0), (0, 0)]
    pad[axis] = (0, LORA_PAD - w.shape[axis])
    return jnp.pad(w, pad).astype(BF16)


def _odd_layer(x2, seq, mu, w_r, w_k, w_v, w_g, w0, w1, w2, a0, a1, a2, k_k, k_a, r_k,
               gn_w, gn_b, w_o, ln_g, ln_b):
    m = x2.shape[0]
    batch = m // seq
    row = lambda vec: vec.reshape(1, D_MODEL)
    r, k, v, g = (_shift_project(x2, row(mu[j]), w, seq, tm=512)
                  for j, w in ((0, w_r), (2, w_k), (3, w_v), (5, w_g)))
    mu2 = jnp.stack([mu[1], mu[4]])
    lw, a = _lora(x2, mu2, _pad_rank(w1, 1), _pad_rank(w2, 0), row(w0),
                  _pad_rank(a1, 1), _pad_rank(a2, 0), row(a0), seq, tm=256)
    tokens = lambda t2: t2.reshape(batch, seq, D_MODEL)
    mixed = _wkv(tokens(r), tokens(k), tokens(v), tokens(g), tokens(lw), tokens(a),
                 row(k_k), row(k_a), row(r_k), row(gn_w), row(gn_b), chunks=4, quads=4)
    return _odd_tail(mixed.reshape(m, D_MODEL), x2, w_o.astype(BF16), row(ln_g), row(ln_b), tm=512)


def kernel(x, ev_w_in, ev_w_pool, ev_pool_scale, ev_w_out, od_mu, od_w_r, od_w_k, od_w_v, od_w_g,
           od_w0, od_w1, od_w2, od_a0, od_a1, od_a2, od_k_k, od_k_a, od_r_k, od_gn_w, od_gn_b,
           od_w_o, ln_g, ln_b):
    batch, seq, d = x.shape
    x2 = x.reshape(batch * seq, d)
    for layer in range(N_LAYERS):
        j = layer // 2
        if layer % 2 == 0:
            x2 = _even_layer(x2, seq, ev_w_in[j], ev_w_pool[j], ev_pool_scale[j], ev_w_out[j],
                             ln_g[layer], ln_b[layer])
        else:
            x2 = _odd_layer(x2, seq, od_mu[j], od_w_r[j], od_w_k[j], od_w_v[j], od_w_g[j],
                            od_w0[j], od_w1[j], od_w2[j], od_a0[j], od_a1[j], od_a2[j],
                            od_k_k[j], od_k_a[j], od_r_k[j], od_gn_w[j], od_gn_b[j], od_w_o[j],
                            ln_g[layer], ln_b[layer])
    return x2.reshape(batch, seq, d)
```

```python
import functools
import math

import jax
import jax.numpy as jnp
from jax import lax
from jax.experimental import pallas as pl
from jax.experimental.pallas import tpu as pltpu

F32 = jnp.float32
BF16 = jnp.bfloat16

D_MODEL = 2048
N_LAYERS = 2
SB_HEADS = 8
SB_HEAD_DIM = 128
SB_WIDTH = SB_HEADS * SB_HEAD_DIM
POOL_WINDOWS = (2, 4, 8, 16)
POOL_GROUP_DIM = 256
POOL_WIDTH = len(POOL_WINDOWS) * POOL_GROUP_DIM
POOL_HALO = 16
RWKV_HEAD_DIM = 64
LORA_PAD = 128
GN_EPS = 64e-5
L2_EPS = 1e-12
LN_EPS = 1e-5
DEEPNORM_ALPHA = (2 * N_LAYERS) ** 0.25

SUBLANES = 8
LANES = 128
VMEM_LIMIT = 52 * 1024 * 1024

EXP_UNDERFLOW = -104.0
DECAY_SCALE = math.exp(-0.5)

WKV_CHUNK = 64
WKV_HEADS = 4
WKV_LANES = WKV_HEADS * RWKV_HEAD_DIM
assert WKV_CHUNK == RWKV_HEAD_DIM


def _dot(a, b):
    return jnp.dot(a.astype(BF16), b.astype(BF16), preferred_element_type=F32)


def _dot_nt(a, b):
    return lax.dot_general(a.astype(BF16), b.astype(BF16), (((1,), (1,)), ((), ())),
                           preferred_element_type=F32)


def _dot_tn(a, b):
    return lax.dot_general(a.astype(BF16), b.astype(BF16), (((0,), (0,)), ((), ())),
                           preferred_element_type=F32)


def _silu(g):
    return g / (1.0 + jnp.exp(-g))


def _layer_norm_rows(z, g, b):
    mean = jnp.mean(z, axis=-1, keepdims=True)
    d = z - mean
    var = jnp.mean(d * d, axis=-1, keepdims=True)
    return d * lax.rsqrt(var + LN_EPS) * g + b


def _params(*sem):
    return pltpu.CompilerParams(dimension_semantics=sem, vmem_limit_bytes=VMEM_LIMIT)


def _proj_kernel(x_ref, w_ref, s_ref, o_ref):
    acc = jnp.dot(x_ref[...].astype(BF16), w_ref[...], preferred_element_type=F32)
    o_ref[...] = (acc * s_ref[...]).astype(o_ref.dtype)


def _project(x, w, first_col, col_scale, out_dtype, tm):
    m, k = x.shape
    n = col_scale.shape[1]
    skip = first_col // n
    return pl.pallas_call(
        _proj_kernel,
        grid=(m // tm,),
        in_specs=[pl.BlockSpec((tm, k), lambda i: (i, 0)),
                  pl.BlockSpec((k, n), lambda i: (0, skip), pipeline_mode=pl.Buffered(1)),
                  pl.BlockSpec((1, n), lambda i: (0, 0))],
        out_specs=pl.BlockSpec((tm, n), lambda i: (i, 0)),
        out_shape=jax.ShapeDtypeStruct((m, n), out_dtype),
        compiler_params=_params("parallel"),
        name="in_proj",
    )(x, w, col_scale)


def _attn_kernel(q_ref, k_ref, v_ref, g_ref, o_ref, *, tq, tk, heads):
    i = pl.program_id(2)
    inner = tq // tk
    krow = lax.broadcasted_iota(jnp.int32, (tk, tk), 0)
    kcol = lax.broadcasted_iota(jnp.int32, (tk, tk), 1)
    after = jnp.where(krow > kcol, 1.0, 0.0).astype(BF16)
    lanes = [slice(h * SB_HEAD_DIM, (h + 1) * SB_HEAD_DIM) for h in range(heads)]

    def block(h, rows, start, carry, acc, diagonal):
        q = q_ref[0, rows, lanes[h]]
        kb = k_ref[0, pl.ds(start, tk), lanes[h]]
        vb = v_ref[0, pl.ds(start, tk), lanes[h]]
        z = _dot_nt(q, kb)
        log_beta = jnp.minimum(z, 0.0) - jnp.log(1.0 + jnp.exp(-jnp.abs(z)))
        log_keep = log_beta - z
        if diagonal:
            causal = (lax.broadcasted_iota(jnp.int32, z.shape, 1)
                      < lax.broadcasted_iota(jnp.int32, z.shape, 0))
            log_keep = jnp.where(causal, log_keep, 0.0)
        later = jnp.dot(log_keep.astype(BF16), after, preferred_element_type=F32) + carry
        w = jnp.exp(log_beta + later)
        if diagonal:
            w = jnp.where(causal, w, 0.0)
        acc = acc + _dot(w, vb)
        carry = carry + jnp.sum(log_keep, axis=-1, keepdims=True)
        return carry, acc

    carry = [jnp.zeros((tq, 1), F32) for _ in range(heads)]
    acc = [jnp.zeros((tq, SB_HEAD_DIM), F32) for _ in range(heads)]
    for jo in reversed(range(inner)):
        lo_row = jo * tk
        start = pl.multiple_of(i * tq + lo_row, tk)
        for h in range(heads):
            c_new, a_new = block(h, slice(lo_row, tq), start, carry[h][lo_row:tq],
                                 acc[h][lo_row:tq], True)
            carry[h] = jnp.concatenate([carry[h][0:lo_row], c_new], axis=0) if lo_row else c_new
            acc[h] = jnp.concatenate([acc[h][0:lo_row], a_new], axis=0) if lo_row else a_new

    def cond(state):
        j, carries, _ = state
        worst = functools.reduce(jnp.maximum, carries)
        return jnp.logical_and(j >= 0, jnp.max(worst) > EXP_UNDERFLOW)

    for lo_row in range(0, tq, tk):
        rows = slice(lo_row, lo_row + tk)

        def body(state, rows=rows):
            j, carries, accs = state
            start = pl.multiple_of(j * tk, tk)
            done = [block(h, rows, start, carries[h], accs[h], False) for h in range(heads)]
            return j - 1, tuple(c for c, _ in done), tuple(a for _, a in done)

        first = (i * inner - 1, tuple(c[rows] for c in carry), tuple(a[rows] for a in acc))
        _, _, parts = lax.while_loop(cond, body, first)
        for h in range(heads):
            gate = _silu(g_ref[0, rows, lanes[h]])
            o_ref[0, rows, lanes[h]] = (parts[h] * gate).astype(o_ref.dtype)


def _attention(qkv, gu, tq, tk, heads):
    b, s, _ = qkv.shape
    width = heads * SB_HEAD_DIM
    groups = SB_HEADS // heads
    kernel = functools.partial(_attn_kernel, tq=tq, tk=tk, heads=heads)
    return pl.pallas_call(
        kernel,
        grid=(b, groups, s // tq),
        in_specs=[pl.BlockSpec((1, tq, width), lambda bi, h, i: (bi, i, h)),
                  pl.BlockSpec((1, s, width), lambda bi, h, i: (bi, 0, groups + h)),
                  pl.BlockSpec((1, s, width), lambda bi, h, i: (bi, 0, 2 * groups + h)),
                  pl.BlockSpec((1, tq, width), lambda bi, h, i: (bi, i, h))],
        out_specs=pl.BlockSpec((1, tq, width), lambda bi, h, i: (bi, i, h)),
        out_shape=jax.ShapeDtypeStruct((b, s, SB_WIDTH), BF16),
        compiler_params=_params("parallel", "parallel", "arbitrary"),
        name="stickbreak_attn",
    )(qkv, qkv, qkv, gu)


def _even_tail_kernel(ma_ref, u_ref, halo_ref, gb_ref, x_ref, wp_ref, ps_ref, wo_ref,
                      lg_ref, lb_ref, o_ref, *, tm, seq):
    i = pl.program_id(0)
    first = (i * tm) % seq == 0
    u = u_ref[...]
    halo = jnp.where(first, 0.0, halo_ref[...])
    ext = jnp.concatenate([halo, u], axis=0)
    pos = (i * tm) % seq + lax.broadcasted_iota(jnp.int32, (tm, 1), 0)

    y = jnp.dot(ma_ref[...], wo_ref[0:SB_WIDTH, :], preferred_element_type=F32)
    for g, window in enumerate(POOL_WINDOWS):
        lo = g * POOL_GROUP_DIM
        hi = lo + POOL_GROUP_DIM
        acc = ext[:, lo:hi]
        span = 1
        while span < window:
            acc = acc + pltpu.roll(acc, span, 0)
            span *= 2
        count = jnp.minimum(pos + 1, window).astype(F32)
        pooled = acc[POOL_HALO:, :] / count - u[:, lo:hi]
        mixed = _dot(pooled, wp_ref[g]) * ps_ref[:, lo:hi]
        mb = mixed * _silu(gb_ref[:, lo:hi])
        y = y + _dot(mb, wo_ref[SB_WIDTH + lo:SB_WIDTH + hi, :])

    z = DEEPNORM_ALPHA * x_ref[...] + y
    o_ref[...] = _layer_norm_rows(z, lg_ref[...], lb_ref[...])


def _even_tail(ma, gu, x, w_pool, pool_scale, w_out, ln_g, ln_b, seq, tm):
    m = x.shape[0]
    halo_blocks = tm // POOL_HALO
    u_col = SB_WIDTH // POOL_WIDTH
    kernel = functools.partial(_even_tail_kernel, tm=tm, seq=seq)
    return pl.pallas_call(
        kernel,
        grid=(m // tm,),
        in_specs=[pl.BlockSpec((tm, SB_WIDTH), lambda i: (i, 0)),
                  pl.BlockSpec((tm, POOL_WIDTH), lambda i: (i, u_col)),
                  pl.BlockSpec((POOL_HALO, POOL_WIDTH),
                               lambda i: (jnp.maximum(i * halo_blocks - 1, 0), u_col)),
                  pl.BlockSpec((tm, POOL_WIDTH), lambda i: (i, u_col + 1)),
                  pl.BlockSpec((tm, D_MODEL), lambda i: (i, 0)),
                  pl.BlockSpec(w_pool.shape, lambda i: (0, 0, 0)),
                  pl.BlockSpec((1, POOL_WIDTH), lambda i: (0, 0)),
                  pl.BlockSpec(w_out.shape, lambda i: (0, 0)),
                  pl.BlockSpec((1, D_MODEL), lambda i: (0, 0)),
                  pl.BlockSpec((1, D_MODEL), lambda i: (0, 0))],
        out_specs=pl.BlockSpec((tm, D_MODEL), lambda i: (i, 0)),
        out_shape=jax.ShapeDtypeStruct((m, D_MODEL), F32),
        compiler_params=_params("parallel"),
        name="even_tail",
    )(ma, gu, gu, gu, x, w_pool, pool_scale, w_out, ln_g, ln_b)


def _shifted_delta(x, halo_ref, first):
    last = jnp.where(first, 0.0, halo_ref[SUBLANES - 1:SUBLANES, :])
    row = lax.broadcasted_iota(jnp.int32, (x.shape[0], 1), 0)
    prev = jnp.where(row == 0, last, pltpu.roll(x, 1, 0))
    return prev - x


def _shift_proj_kernel(x_ref, halo_ref, mu_ref, w_ref, o_ref, wb_ref, *, tm, seq):
    i = pl.program_id(0)

    @pl.when(i == 0)
    def _():
        wb_ref[...] = w_ref[...].astype(BF16)

    x = x_ref[...]
    xx = _shifted_delta(x, halo_ref, (i * tm) % seq == 0)
    o_ref[...] = _dot(x + xx * mu_ref[...], wb_ref[...])


def _shift_project(x, mu, w, seq, tm):
    m = x.shape[0]
    halo_blocks = tm // SUBLANES
    kernel = functools.partial(_shift_proj_kernel, tm=tm, seq=seq)
    return pl.pallas_call(
        kernel,
        grid=(m // tm,),
        in_specs=[pl.BlockSpec((tm, D_MODEL), lambda i: (i, 0)),
                  pl.BlockSpec((SUBLANES, D_MODEL),
                               lambda i: (jnp.maximum(i * halo_blocks - 1, 0), 0)),
                  pl.BlockSpec((1, D_MODEL), lambda i: (0, 0)),
                  pl.BlockSpec((D_MODEL, D_MODEL), lambda i: (0, 0), pipeline_mode=pl.Buffered(1))],
        out_specs=pl.BlockSpec((tm, D_MODEL), lambda i: (i, 0)),
        out_shape=jax.ShapeDtypeStruct((m, D_MODEL), F32),
        scratch_shapes=[pltpu.VMEM((D_MODEL, D_MODEL), BF16)],
        compiler_params=_params("arbitrary"),
        name="shift_proj",
    )(x, x, mu, w)


def _lora_kernel(x_ref, halo_ref, mu_ref, w1_ref, w2_ref, w0_ref, a1_ref, a2_ref, a0_ref,
                 lw_ref, a_ref, *, tm, seq):
    i = pl.program_id(0)
    x = x_ref[...]
    xx = _shifted_delta(x, halo_ref, (i * tm) % seq == 0)
    xw = x + xx * mu_ref[0:1, :]
    xa = x + xx * mu_ref[1:2, :]
    ww = w0_ref[...] + _dot(jnp.tanh(_dot(xw, w1_ref[...])), w2_ref[...])
    lw_ref[...] = -DECAY_SCALE / (1.0 + jnp.exp(-ww))
    aa = a0_ref[...] + _dot(_dot(xa, a1_ref[...]), a2_ref[...])
    a_ref[...] = 1.0 / (1.0 + jnp.exp(-aa))


def _lora(x, mu2, w1, w2, w0, a1, a2, a0, seq, tm):
    m = x.shape[0]
    halo_blocks = tm // SUBLANES
    kernel = functools.partial(_lora_kernel, tm=tm, seq=seq)
    full = lambda arr: pl.BlockSpec(arr.shape, lambda i: (0,) * arr.ndim)
    out = jax.ShapeDtypeStruct((m, D_MODEL), F32)
    return pl.pallas_call(
        kernel,
        grid=(m // tm,),
        in_specs=[pl.BlockSpec((tm, D_MODEL), lambda i: (i, 0)),
                  pl.BlockSpec((SUBLANES, D_MODEL),
                               lambda i: (jnp.maximum(i * halo_blocks - 1, 0), 0)),
                  full(mu2), full(w1), full(w2), full(w0), full(a1), full(a2), full(a0)],
        out_specs=[pl.BlockSpec((tm, D_MODEL), lambda i: (i, 0)),
                   pl.BlockSpec((tm, D_MODEL), lambda i: (i, 0))],
        out_shape=[out, out],
        compiler_params=_params("parallel"),
        name="decay_iclr_lora",
    )(x, x, mu2, w1, w2, w0, a1, a2, a0)


def _blockdiag(y, half_masks):
    yb = y.astype(BF16)
    zero = jnp.zeros((y.shape[0], LANES), BF16)
    tiles = y.shape[1] // LANES
    rows = []
    for h in range(WKV_HEADS):
        mine = h // 2
        kept = yb[:, mine * LANES:(mine + 1) * LANES] * half_masks[h % 2]
        rows.append(jnp.concatenate([kept if tl == mine else zero for tl in range(tiles)], axis=1))
    return jnp.concatenate(rows, axis=0)


def _fold_diag_blocks(full):
    c = WKV_CHUNK
    first_head = lax.broadcasted_iota(jnp.int32, (1, LANES), 1) < RWKV_HEAD_DIM
    tiles = []
    for tl in range(full.shape[1] // LANES):
        cols = slice(tl * LANES, (tl + 1) * LANES)
        tiles.append(jnp.where(first_head, full[2 * tl * c:(2 * tl + 1) * c, cols],
                               full[(2 * tl + 1) * c:(2 * tl + 2) * c, cols]))
    return jnp.concatenate(tiles, axis=1)


def _wkv_kernel(r_ref, k_ref, v_ref, g_ref, lw_ref, a_ref, kk_ref, ka_ref, rk_ref,
                gw_ref, gb_ref, o_ref, state_ref, *, chunks, quads):
    @pl.when(pl.program_id(2) == 0)
    def _():
        state_ref[...] = jnp.zeros_like(state_ref)

    c = WKV_CHUNK
    w = WKV_LANES
    t = chunks * c
    srow = lax.broadcasted_iota(jnp.int32, (w, w), 0)
    scol = lax.broadcasted_iota(jnp.int32, (w, w), 1)
    bd_mask = jnp.where((srow // c) == (scol // c), 1.0, 0.0).astype(BF16)
    crow = lax.broadcasted_iota(jnp.int32, (c, w), 0)
    ccol = lax.broadcasted_iota(jnp.int32, (c, w), 1) % c
    strict = crow > ccol
    lower = crow >= ccol
    eye_cat = jnp.where(crow == ccol, 1.0, 0.0)
    tg = min(t, w)
    trow = lax.broadcasted_iota(jnp.int32, (tg, tg), 0)
    tcol = lax.broadcasted_iota(jnp.int32, (tg, tg), 1)
    prefix = jnp.where(jnp.logical_and((trow // c) == (tcol // c), trow >= tcol),
                       1.0, 0.0).astype(BF16)

    def prepare(qi):
        lanes = slice(qi * w, (qi + 1) * w)
        r = r_ref[0, :, lanes]
        k = k_ref[0, :, lanes]
        v = v_ref[0, :, lanes]
        lw = lw_ref[0, :, lanes]
        a = a_ref[0, :, lanes]

        kk = k * kk_ref[:, lanes]
        kmod = k * (1.0 + (a - 1.0) * ka_ref[:, lanes])
        sums = _dot(jnp.concatenate([kk * kk, r * kmod * rk_ref[:, lanes]], axis=0), bd_mask)
        kk = kk / jnp.maximum(jnp.sqrt(sums[0:t]), L2_EPS)
        b = a * kk

        lw_hi = lw.astype(BF16)
        lw_lo = (lw - lw_hi.astype(F32)).astype(BF16)
        lw_split = jnp.concatenate([lw_hi, lw_lo], axis=1)
        cum2 = jnp.concatenate([jnp.dot(prefix, lw_split[s:s + tg], preferred_element_type=F32)
                                for s in range(0, t, tg)], axis=0)
        cum = cum2[:, 0:w] + cum2[:, w:2 * w]
        totals = [cum[(ci + 1) * c - 1:(ci + 1) * c, :] for ci in range(chunks)]
        g_in = jnp.exp(cum)
        g_inv = 1.0 / g_in
        g_end = jnp.concatenate([jnp.broadcast_to(jnp.exp(tot), (c, w)) for tot in totals], axis=0)
        g_tail = g_end * g_inv
        whole = dict(kkg=kk * jnp.exp(cum - lw), rg=r * g_in, kd=kmod * g_inv,
                     bd=b * g_inv, kt=kmod * g_tail, bt=b * g_tail, v=v)
        per_chunk = [{name: x[ci * c:(ci + 1) * c] for name, x in whole.items()}
                     for ci in range(chunks)]
        for ci in range(chunks):
            per_chunk[ci]["total"] = totals[ci]
        return per_chunk, sums[t:2 * t] * v

    prepared = [prepare(qi) for qi in range(quads)]
    units = [(qi, ci) for ci in range(chunks) for qi in range(quads)]
    every = range(len(units))
    un = [prepared[qi][0][ci] for qi, ci in units]
    tile_lane = lax.broadcasted_iota(jnp.int32, (1, LANES), 1)
    half_masks = [jnp.where(tile_lane < RWKV_HEAD_DIM, 1.0, 0.0).astype(BF16),
                  jnp.where(tile_lane < RWKV_HEAD_DIM, 0.0, 1.0).astype(BF16)]
    bdm = lambda y: _blockdiag(y, half_masks)

    lhs = [jnp.concatenate([un[ci]["kkg"], un[ci]["rg"]], axis=0) for ci in every]
    a1 = [_dot_nt(lhs[ci], bdm(un[ci]["bd"])) for ci in every]
    a2 = [_dot_nt(lhs[ci], bdm(un[ci]["kd"])) for ci in every]
    a_kb = [jnp.where(strict, a1[ci][0:c], 0.0) for ci in every]
    a_rb = [jnp.where(lower, a1[ci][c:2 * c], 0.0) for ci in every]
    a_kk = [jnp.where(strict, a2[ci][0:c], 0.0) for ci in every]
    a_rk = [jnp.where(lower, a2[ci][c:2 * c], 0.0) for ci in every]

    power = [-a_kb[ci] for ci in every]
    inv = [eye_cat + power[ci] for ci in every]
    power = [_dot(power[ci], bdm(power[ci])) for ci in every]
    span = 2
    while 2 * span < c:
        both = [_dot(jnp.concatenate([inv[ci], power[ci]], axis=0), bdm(power[ci])) for ci in every]
        inv = [inv[ci] + both[ci][0:c] for ci in every]
        power = [both[ci][c:2 * c] for ci in every]
        span *= 2
    inv = [inv[ci] + _dot(inv[ci], bdm(power[ci])) for ci in every]

    av = [_dot(jnp.concatenate([a_kk[ci], a_rk[ci]], axis=0), bdm(un[ci]["v"])) for ci in every]
    akkv = [av[ci][0:c] for ci in every]
    arkv = [av[ci][c:2 * c] for ci in every]
    wu = [_dot(inv[ci], jnp.concatenate([bdm(un[ci]["kkg"]), bdm(akkv[ci])], axis=1)) for ci in every]
    w1 = [wu[ci][:, 0:w] for ci in every]
    u0 = [wu[ci][:, w:2 * w] for ci in every]
    arb = [_dot(a_rb[ci], jnp.concatenate([bdm(w1[ci]), bdm(u0[ci])], axis=1)) for ci in every]
    r2 = [un[ci]["rg"] - arb[ci][:, 0:w] for ci in every]
    o0 = [arkv[ci] - arb[ci][:, w:2 * w] for ci in every]
    decay = [jnp.where(crow == ccol, jnp.exp(un[ci]["total"]), 0.0) for ci in every]
    kb_t = [jnp.concatenate([un[ci]["kt"], -un[ci]["bt"]], axis=0) for ci in every]

    states = [state_ref[qi] for qi in range(quads)]
    outs = [[] for _ in range(quads)]
    for ui, (qi, _) in enumerate(units):
        three = _dot(jnp.concatenate([w1[ui], r2[ui], decay[ui]], axis=0), bdm(states[qi]))
        u_full = three[0:c] + u0[ui]
        outs[qi].append(three[c:2 * c] + o0[ui])
        states[qi] = three[2 * c:3 * c] + _fold_diag_blocks(
            _dot_tn(kb_t[ui], jnp.concatenate([un[ui]["v"], u_full], axis=0)))

    inv_n = 1.0 / RWKV_HEAD_DIM
    for qi in range(quads):
        lanes = slice(qi * w, (qi + 1) * w)
        state_ref[qi] = states[qi]
        o = jnp.concatenate(outs[qi], axis=0)
        d = o - _dot(o, bd_mask) * inv_n
        var = _dot(d * d, bd_mask) * inv_n
        normed = d * lax.rsqrt(var + GN_EPS) * gw_ref[:, lanes] + gb_ref[:, lanes]
        gate = _silu(g_ref[0, :, lanes])
        o_ref[0, :, lanes] = ((normed + prepared[qi][1]) * gate).astype(o_ref.dtype)


def _wkv(r, k, v, g, lw, a, k_k, k_a, r_k, gn_w, gn_b, chunks, quads):
    b, s, d = r.shape
    t = chunks * WKV_CHUNK
    width = quads * WKV_LANES
    kernel = functools.partial(_wkv_kernel, chunks=chunks, quads=quads)
    tok = pl.BlockSpec((1, t, width), lambda bi, h, ci: (bi, ci, h))
    vec = pl.BlockSpec((1, width), lambda bi, h, ci: (0, h))
    return pl.pallas_call(
        kernel,
        grid=(b, d // width, s // t),
        in_specs=[tok, tok, tok, tok, tok, tok, vec, vec, vec, vec, vec],
        out_specs=tok,
        out_shape=jax.ShapeDtypeStruct((b, s, d), BF16),
        scratch_shapes=[pltpu.VMEM((quads, RWKV_HEAD_DIM, WKV_LANES), F32)],
        compiler_params=_params("parallel", "parallel", "arbitrary"),
        name="wkv7_chunked",
    )(r, k, v, g, lw, a, k_k, k_a, r_k, gn_w, gn_b)


def _odd_tail_kernel(m_ref, x_ref, wo_ref, lg_ref, lb_ref, o_ref, *, parts):
    rows_per = m_ref.shape[0] // parts
    for p in range(parts):
        rows = slice(p * rows_per, (p + 1) * rows_per)
        y = jnp.dot(m_ref[rows, :], wo_ref[...], preferred_element_type=F32)
        z = DEEPNORM_ALPHA * x_ref[rows, :] + y
        o_ref[rows, :] = _layer_norm_rows(z, lg_ref[...], lb_ref[...])


def _odd_tail(mixed, x, w_o, ln_g, ln_b, tm):
    m = x.shape[0]
    return pl.pallas_call(
        functools.partial(_odd_tail_kernel, parts=4),
        grid=(m // tm,),
        in_specs=[pl.BlockSpec((tm, D_MODEL), lambda i: (i, 0)),
                  pl.BlockSpec((tm, D_MODEL), lambda i: (i, 0)),
                  pl.BlockSpec(w_o.shape, lambda i: (0, 0)),
                  pl.BlockSpec((1, D_MODEL), lambda i: (0, 0)),
                  pl.BlockSpec((1, D_MODEL), lambda i: (0, 0))],
        out_specs=pl.BlockSpec((tm, D_MODEL), lambda i: (i, 0)),
        out_shape=jax.ShapeDtypeStruct((m, D_MODEL), F32),
        compiler_params=_params("parallel"),
        name="odd_tail",
    )(mixed, x, w_o, ln_g, ln_b)


def _even_layer(x2, seq, w_in, w_pool, pool_scale, w_out, ln_g, ln_b):
    m = x2.shape[0]
    batch = m // seq
    split = 3 * SB_WIDTH
    w_in = w_in.astype(BF16)
    q_scale = jnp.concatenate([jnp.full((1, SB_WIDTH), SB_HEAD_DIM ** -0.5, F32),
                               jnp.ones((1, split - SB_WIDTH), F32)], axis=1)
    qkv = _project(x2, w_in, 0, q_scale, BF16, tm=512)
    gu = _project(x2, w_in, split, jnp.ones((1, split), F32), F32, tm=512)
    ma = _attention(qkv.reshape(batch, seq, split), gu.reshape(batch, seq, split),
                    tq=512, tk=256, heads=8)
    return _even_tail(ma.reshape(m, SB_WIDTH), gu, x2, w_pool.astype(BF16),
                      pool_scale.reshape(1, POOL_WIDTH), w_out.astype(BF16),
                      ln_g.reshape(1, D_MODEL), ln_b.reshape(1, D_MODEL), seq, tm=512)


def _pad_rank(w, axis):
    pad = [(0, 0), (0, 0)]
    pad[axis] = (0, LORA_PAD - w.shape[axis])
    return jnp.pad(w, pad).astype(BF16)


def _odd_layer(x2, seq, mu, w_r, w_k, w_v, w_g, w0, w1, w2, a0, a1, a2, k_k, k_a, r_k,
               gn_w, gn_b, w_o, ln_g, ln_b):
    m = x2.shape[0]
    batch = m // seq
    row = lambda vec: vec.reshape(1, D_MODEL)
    r, k, v, g = (_shift_project(x2, row(mu[j]), w, seq, tm=512)
                  for j, w in ((0, w_r), (2, w_k), (3, w_v), (5, w_g)))
    mu2 = jnp.stack([mu[1], mu[4]])
    lw, a = _lora(x2, mu2, _pad_rank(w1, 1), _pad_rank(w2, 0), row(w0),
                  _pad_rank(a1, 1), _pad_rank(a2, 0), row(a0), seq, tm=256)
    tokens = lambda t2: t2.reshape(batch, seq, D_MODEL)
    mixed = _wkv(tokens(r), tokens(k), tokens(v), tokens(g), tokens(lw), tokens(a),
                 row(k_k), row(k_a), row(r_k), row(gn_w), row(gn_b), chunks=4, quads=4)
    return _odd_tail(mixed.reshape(m, D_MODEL), x2, w_o.astype(BF16), row(ln_g), row(ln_b), tm=512)


def kernel(x, ev_w_in, ev_w_pool, ev_pool_scale, ev_w_out, od_mu, od_w_r, od_w_k, od_w_v, od_w_g,
           od_w0, od_w1, od_w2, od_a0, od_a1, od_a2, od_k_k, od_k_a, od_r_k, od_gn_w, od_gn_b,
           od_w_o, ln_g, ln_b):
    batch, seq, d = x.shape
    x2 = x.reshape(batch * seq, d)
    for layer in range(N_LAYERS):
        j = layer // 2
        if layer % 2 == 0:
            x2 = _even_layer(x2, seq, ev_w_in[j], ev_w_pool[j], ev_pool_scale[j], ev_w_out[j],
                             ln_g[layer], ln_b[layer])
        else:
            x2 = _odd_layer(x2, seq, od_mu[j], od_w_r[j], od_w_k[j], od_w_v[j], od_w_g[j],
                            od_w0[j], od_w1[j], od_w2[j], od_a0[j], od_a1[j], od_a2[j],
                            od_k_k[j], od_k_a[j], od_r_k[j], od_gn_w[j], od_gn_b[j], od_w_o[j],
                            ln_g[layer], ln_b[layer])
    return x2.reshape(batch, seq, d)
```
